```python
import math
import jax
import jax.numpy as jnp
from jax import lax
import numpy as np

D_MODEL = 1024
BATCH = 4
SEQ = 4096
DEPTH = 2

GRID_W = 64
CTX_LEN = 256
HEAD_DIM = 64
ATT_Q_HEADS = 8
ATT_KV_HEADS = 2
ATT_GROUP = ATT_Q_HEADS // ATT_KV_HEADS
ATT_W = ATT_Q_HEADS * HEAD_DIM
KV_W = ATT_KV_HEADS * HEAD_DIM
Q_BLOCK = 128
ROPE_THETA = 10000.0
HY_W = 256
HY_SHORT = 3
HY_EMB = 33
HY_BANDS = (HY_EMB - 1) // 2
HY_FO = 64
HY_DECAY_TARGET = 1e-2
HY_FAST_DECAY_PCT = 0.3
HY_SLOW_DECAY_PCT = 1.5
HY_MIN_DECAY = math.log(HY_DECAY_TARGET) / HY_SLOW_DECAY_PCT
HY_MAX_DECAY = math.log(HY_DECAY_TARGET) / HY_FAST_DECAY_PCT
NA_HEADS = 4
NA_W = NA_HEADS * HEAD_DIM
NA_WIN_ROWS = 8
NA_WIN_COLS = 16
D_MIX = ATT_W + HY_W + NA_W
D_IN = ATT_W + 2 * KV_W + 3 * HY_W + 3 * NA_W
SPLIT_POINTS = (ATT_W, ATT_W + KV_W, ATT_W + 2 * KV_W, ATT_W + 2 * KV_W + 3 * HY_W,
                ATT_W + 2 * KV_W + 3 * HY_W + NA_W, ATT_W + 2 * KV_W + 3 * HY_W + 2 * NA_W)
N_EXPERTS = 32
TOP_K = 4
D_EXPERT = D_MODEL
SWIGLU_ALPHA = 1.702
SWIGLU_LIMIT = 7.0
DEEPNORM_ALPHA = (2.0 * DEPTH) ** 0.25
DEEPNORM_BETA = (8.0 * DEPTH) ** -0.25
NORM_EPS = 1e-6

kernel_name = 'hybrid_gqa_hyena_natten_moe_dit'


def layer_norm(x, g=None, b=None):
    xf = x.astype(jnp.float32)
    mu = jnp.mean(xf, axis=-1, keepdims=True)
    var = jnp.mean(jnp.square(xf - mu), axis=-1, keepdims=True)
    y = (xf - mu) * lax.rsqrt(var + NORM_EPS)
    if g is not None:
        y = y * g + b
    return y.astype(x.dtype)


def rms_norm(x, g):
    xf = x.astype(jnp.float32)
    y = xf * lax.rsqrt(jnp.mean(jnp.square(xf), axis=-1, keepdims=True) + NORM_EPS) * g
    return y.astype(x.dtype)


def modulate(x, shift, scale):
    return layer_norm(x) * (1.0 + scale) + shift


def heads(t, n):
    return t.reshape(t.shape[0], t.shape[1], n, HEAD_DIM)


def rope_2d_tables(n_tokens):
    t = jnp.arange(n_tokens)
    row = (t // GRID_W).astype(jnp.float32)[:, None]
    col = (t % GRID_W).astype(jnp.float32)[:, None]
    axis_dim = HEAD_DIM // 2
    inv_freq = ROPE_THETA ** (-jnp.arange(0, axis_dim, 2, dtype=jnp.float32) / axis_dim)
    ang_r = row * inv_freq
    ang_c = col * inv_freq
    ang = jnp.concatenate([ang_r, ang_r, ang_c, ang_c], axis=-1)[:, None, :]
    return jnp.cos(ang), jnp.sin(ang)


def apply_rope_2d(x, cos, sin):
    xf = x.astype(jnp.float32)
    xs = xf.reshape(*x.shape[:-1], 2, 2, HEAD_DIM // 4)
    rot = jnp.stack([-xs[..., 1, :], xs[..., 0, :]], axis=-2).reshape(x.shape)
    return (xf * cos + rot * sin).astype(x.dtype)


def gqa_attention(q, k, v):
    s = jnp.einsum('blkgd,bskd->bkgls', q, k).astype(jnp.float32)
    p = jax.nn.softmax(s, axis=-1).astype(v.dtype)
    o = jnp.einsum('bkgls,bskd->blkgd', p, v)
    return o.reshape(o.shape[0], o.shape[1], -1)


def gqa_block_sweep(q, k, v):
    B, L = q.shape[:2]
    nb = L // Q_BLOCK
    qb = jnp.moveaxis(q.reshape(B, nb, Q_BLOCK, *q.shape[2:]), 1, 0)
    o = lax.map(lambda qq: gqa_attention(qq, k, v), qb)
    return jnp.moveaxis(o, 0, 1).reshape(B, L, -1)


def neighbourhood_attention(q, k, v, k_ctx, v_ctx, rpb):
    B, N, H, hd = q.shape
    rows = N // GRID_W
    wr = min(NA_WIN_ROWS, rows)
    wc = NA_WIN_COLS
    qg = jnp.moveaxis(q.reshape(B, rows, GRID_W, H, hd), 1, 0) * (hd ** -0.5)
    kg = k.reshape(B, rows, GRID_W, H, hd)
    vg = v.reshape(B, rows, GRID_W, H, hd)
    col = jnp.arange(GRID_W)
    col_idx = jnp.clip(col - wc // 2, 0, GRID_W - wc)[:, None] + jnp.arange(wc)[None, :]
    rel_col = col_idx - col[:, None] + (NA_WIN_COLS - 1)
    rpb_cols = rpb[:, :, rel_col]

    def row_block(args):
        r, qr = args
        rs = jnp.clip(r - wr // 2, 0, rows - wr)
        kw = lax.dynamic_slice_in_dim(kg, rs, wr, axis=1)[:, :, col_idx]
        vw = lax.dynamic_slice_in_dim(vg, rs, wr, axis=1)[:, :, col_idx]
        rel_row = rs + jnp.arange(wr) - r + (NA_WIN_ROWS - 1)
        bias = jnp.moveaxis(rpb_cols[:, rel_row], 1, 2).astype(jnp.float32)
        s_loc = jnp.einsum('bqhd,biqjhd->bhqij', qr, kw).astype(jnp.float32) + bias
        s_ctx = jnp.einsum('bqhd,bmhd->bhqm', qr, k_ctx).astype(jnp.float32)
        s = jnp.concatenate([s_loc.reshape(B, H, GRID_W, wr * wc), s_ctx], axis=-1)
        p = jax.nn.softmax(s, axis=-1).astype(v.dtype)
        p_loc = p[..., :wr * wc].reshape(B, H, GRID_W, wr, wc)
        return (jnp.einsum('bhqij,biqjhd->bqhd', p_loc, vw)
                + jnp.einsum('bhqm,bmhd->bqhd', p[..., wr * wc:], v_ctx))

    o = lax.map(row_block, (jnp.arange(rows), qg))
    return jnp.moveaxis(o, 0, 1).reshape(B, N, H * hd)


def short_conv(u, w, b):
    L = u.shape[1]
    pad = HY_SHORT // 2
    up = jnp.pad(u, ((0, 0), (pad, HY_SHORT - 1 - pad), (0, 0)))
    return sum(up[:, j:j + L] * w[j] for j in range(HY_SHORT)) + b


def hyena_filter(L, f_w1, f_b1, f_freq, f_w2, f_b2, f_w3, f_b3):
    f32 = jnp.float32
    t = jnp.linspace(0.0, 1.0, L, dtype=f32)[:, None]
    w = (2.0 * math.pi / L) * jnp.arange(L, dtype=f32)[:, None]
    bands = jnp.linspace(1e-4, HY_BANDS - 1, HY_BANDS, dtype=f32)[None, :]
    z = jnp.concatenate([t, jnp.cos(bands * w), -jnp.sin(bands * w)], axis=-1)
    hid = jnp.sin(f_freq[0] * (z @ f_w1 + f_b1))
    hid = jnp.sin(f_freq[1] * (hid @ f_w2 + f_b2))
    filt = (hid @ f_w3 + f_b3).astype(f32).reshape(L, 2, HY_W)
    deltas = jnp.linspace(HY_MIN_DECAY, HY_MAX_DECAY, HY_W, dtype=f32)
    filt = filt * jnp.exp(-t * jnp.abs(deltas))[:, None, :]
    h_fwd, h_bwd = filt[:, 0], filt[:, 1]
    g = jnp.concatenate([h_fwd, jnp.zeros((1, HY_W), f32), h_bwd[:0:-1]], axis=0)
    return g * lax.rsqrt(jnp.sum(g * g, axis=0, keepdims=True) + NORM_EPS)


def hyena_mixer(p, conv_w, conv_b, f_w1, f_b1, f_freq, f_w2, f_b2, f_w3, f_b3, d_skip):
    L = p.shape[1]
    x0, x1, v = jnp.split(short_conv(p, conv_w, conv_b), 3, axis=-1)
    g = hyena_filter(L, f_w1, f_b1, f_freq, f_w2, f_b2, f_w3, f_b3)
    z = (x1 * v).astype(jnp.float32)
    zf = jnp.fft.rfft(z, n=2 * L, axis=1)
    gf = jnp.fft.rfft(g, n=2 * L, axis=0)
    y = jnp.fft.irfft(zf * gf[None], n=2 * L, axis=1)[:, :L] + z * d_skip
    return (x0 * y).astype(p.dtype)


def token_mixer(h, hc, need_ctx, w_in, w_out, q_gain, k_gain, conv_w, conv_b, f_w1, f_b1, f_freq,
                f_w2, f_b2, f_w3, f_b3, d_skip, rpb, cos, sin):
    B, N, _ = h.shape
    M = hc.shape[1]
    scale = HEAD_DIM ** -0.5
    a_q, a_k, a_v, hy_p, n_q, n_k, n_v = jnp.split(h @ w_in, SPLIT_POINTS, axis=-1)
    if need_ctx:
        ca_q, ca_k, ca_v, chy_p, cn_q, cn_k, cn_v = jnp.split(hc @ w_in, SPLIT_POINTS, axis=-1)
    else:
        ca_k, ca_v = jnp.split(hc @ w_in[:, ATT_W:ATT_W + 2 * KV_W], 2, axis=-1)
        cn_k, cn_v = jnp.split(hc @ w_in[:, D_IN - 2 * NA_W:], 2, axis=-1)
    q = apply_rope_2d(rms_norm(heads(a_q, ATT_Q_HEADS), q_gain), cos, sin) * scale
    k = apply_rope_2d(rms_norm(heads(a_k, ATT_KV_HEADS), k_gain), cos, sin)
    ck = rms_norm(heads(ca_k, ATT_KV_HEADS), k_gain)
    cv = heads(ca_v, ATT_KV_HEADS)
    k_all = jnp.concatenate([k, ck], axis=1)
    v_all = jnp.concatenate([heads(a_v, ATT_KV_HEADS), cv], axis=1)
    y_att = gqa_block_sweep(q.reshape(B, N, ATT_KV_HEADS, ATT_GROUP, HEAD_DIM), k_all, v_all)
    y_hy = hyena_mixer(hy_p, conv_w, conv_b, f_w1, f_b1, f_freq, f_w2, f_b2, f_w3, f_b3, d_skip)
    nk_c = heads(cn_k, NA_HEADS)
    nv_c = heads(cn_v, NA_HEADS)
    y_na = neighbourhood_attention(heads(n_q, NA_HEADS), heads(n_k, NA_HEADS), heads(n_v, NA_HEADS),
                                   nk_c, nv_c, rpb)
    y = jnp.concatenate([y_att, y_hy, y_na], axis=-1) @ w_out
    if not need_ctx:
        return y, None
    cq = rms_norm(heads(ca_q, ATT_Q_HEADS), q_gain) * scale
    yc_att = gqa_attention(cq.reshape(B, M, ATT_KV_HEADS, ATT_GROUP, HEAD_DIM), ck, cv)
    yc_hy = hyena_mixer(chy_p, conv_w, conv_b, f_w1, f_b1, f_freq, f_w2, f_b2, f_w3, f_b3, d_skip)
    yc_na = gqa_attention(heads(cn_q, NA_HEADS)[:, :, :, None] * scale, nk_c, nv_c)
    yc = jnp.concatenate([yc_att, yc_hy, yc_na], axis=-1) @ w_out
    return y, yc


def expert_ffn(t, router_w, router_b, w1, b1, w2, b2):
    logits = (t @ router_w + router_b).astype(jnp.float32)
    top_v, top_i = lax.top_k(logits, TOP_K)
    wts = jax.nn.softmax(top_v, axis=-1)
    gates = jnp.sum(jax.nn.one_hot(top_i, N_EXPERTS, dtype=jnp.float32) * wts[..., None], axis=1)
    gates = gates.astype(t.dtype)
    out = jnp.zeros_like(t)
    for e in range(N_EXPERTS):
        hid = t @ w1[e] + b1[e]
        glu = jnp.minimum(hid[:, :D_EXPERT], SWIGLU_LIMIT)
        lin = jnp.clip(hid[:, D_EXPERT:], -SWIGLU_LIMIT, SWIGLU_LIMIT)
        act = glu * jax.nn.sigmoid(SWIGLU_ALPHA * glu) * (lin + 1.0)
        out = out + gates[:, e:e + 1] * (act @ w2[e] + b2[e])
    return out


def setup_inputs(seed: int = 0) -> dict:
    key = jax.random.key(seed)
    ks = iter(jax.random.split(key, 31))

    def nrm(shape, scale):
        return scale * jax.random.normal(next(ks), shape, jnp.float32)

    L = DEPTH
    return {
        'x': nrm((BATCH, SEQ, D_MODEL), 1.0),
        'c': nrm((BATCH, D_MODEL), 1.0),
        'ctx': nrm((BATCH, CTX_LEN, D_MODEL), 1.0),
        'c_ctx': nrm((D_MODEL,), 1.0),
        'ada_w': nrm((L, D_MODEL, 6 * D_MODEL), 0.5 * D_MODEL ** -0.5),
        'ada_b': nrm((L, 6 * D_MODEL), 0.01),
        'w_in': nrm((L, D_MODEL, D_IN), D_MODEL ** -0.5),
        'w_out': nrm((L, D_MIX, D_MODEL), DEEPNORM_BETA * D_MIX ** -0.5),
        'q_gain': 1.0 + nrm((L, HEAD_DIM), 0.01),
        'k_gain': 1.0 + nrm((L, HEAD_DIM), 0.01),
        'hy_conv_w': nrm((L, HY_SHORT, 3 * HY_W), HY_SHORT ** -0.5),
        'hy_conv_b': nrm((L, 3 * HY_W), 0.01),
        'hy_w1': nrm((L, HY_EMB, HY_FO), HY_EMB ** -0.5),
        'hy_b1': nrm((L, HY_FO), 0.01),
        'hy_freq': 1.0 + nrm((L, 2, HY_FO), 0.01),
        'hy_w2': nrm((L, HY_FO, HY_FO), HY_FO ** -0.5),
        'hy_b2': nrm((L, HY_FO), 0.01),
        'hy_w3': nrm((L, HY_FO, 2 * HY_W), HY_FO ** -0.5),
        'hy_b3': nrm((L, 2 * HY_W), 0.01),
        'hy_d': nrm((L, HY_W), 0.5),
        'na_rpb': nrm((L, NA_HEADS, 2 * NA_WIN_ROWS - 1, 2 * NA_WIN_COLS - 1), 0.05),
        'ln1_g': 1.0 + nrm((L, D_MODEL), 0.01),
        'ln1_b': nrm((L, D_MODEL), 0.01),
        'ln2_g': 1.0 + nrm((L, D_MODEL), 0.01),
        'ln2_b': nrm((L, D_MODEL), 0.01),
        'router_w': nrm((L, D_MODEL, N_EXPERTS), D_MODEL ** -0.5),
        'router_b': nrm((L, N_EXPERTS), 0.01),
        'exp_w1': nrm((L, N_EXPERTS, D_MODEL, 2 * D_EXPERT), D_MODEL ** -0.5),
        'exp_b1': nrm((L, N_EXPERTS, 2 * D_EXPERT), 0.01),
        'exp_w2': nrm((L, N_EXPERTS, D_EXPERT, D_MODEL), DEEPNORM_BETA * D_EXPERT ** -0.5),
        'exp_b2': nrm((L, N_EXPERTS, D_MODEL), 0.01),
    }


def reference(x, c, ctx, c_ctx, ada_w, ada_b, w_in, w_out, q_gain, k_gain, hy_conv_w, hy_conv_b,
              hy_w1, hy_b1, hy_freq, hy_w2, hy_b2, hy_w3, hy_b3, hy_d, na_rpb, ln1_g, ln1_b,
              ln2_g, ln2_b, router_w, router_b, exp_w1, exp_b1, exp_w2, exp_b2):
    B, N, D = x.shape
    M = ctx.shape[1]
    cos, sin = rope_2d_tables(N)
    c_act = jax.nn.silu(c)
    cc_act = jax.nn.silu(c_ctx)
    for l in range(DEPTH):
        need_ctx = l < DEPTH - 1
        mod = jnp.split((c_act @ ada_w[l] + ada_b[l])[:, None, :], 6, axis=-1)
        cmod = jnp.split(cc_act @ ada_w[l] + ada_b[l], 6, axis=-1)
        y, yc = token_mixer(modulate(x, mod[0], mod[1]), modulate(ctx, cmod[0], cmod[1]), need_ctx,
                            w_in[l], w_out[l], q_gain[l], k_gain[l], hy_conv_w[l], hy_conv_b[l],
                            hy_w1[l], hy_b1[l], hy_freq[l], hy_w2[l], hy_b2[l], hy_w3[l], hy_b3[l],
                            hy_d[l], na_rpb[l], cos, sin)
        x = layer_norm(DEEPNORM_ALPHA * x + mod[2] * y, ln1_g[l], ln1_b[l])
        if need_ctx:
            ctx = layer_norm(DEEPNORM_ALPHA * ctx + cmod[2] * yc, ln1_g[l], ln1_b[l])
            h = modulate(x, mod[3], mod[4]).reshape(B * N, D)
            hc = modulate(ctx, cmod[3], cmod[4]).reshape(B * M, D)
            out = expert_ffn(jnp.concatenate([h, hc], axis=0), router_w[l], router_b[l],
                             exp_w1[l], exp_b1[l], exp_w2[l], exp_b2[l])
            y = out[:B * N].reshape(B, N, D)
            ctx = layer_norm(DEEPNORM_ALPHA * ctx + cmod[5] * out[B * N:].reshape(B, M, D),
                             ln2_g[l], ln2_b[l])
        else:
            y = expert_ffn(modulate(x, mod[3], mod[4]).reshape(B * N, D), router_w[l], router_b[l],
                           exp_w1[l], exp_b1[l], exp_w2[l], exp_b2[l]).reshape(B, N, D)
        x = layer_norm(DEEPNORM_ALPHA * x + mod[5] * y, ln2_g[l], ln2_b[l])
    return x
```

```python
import cmath
import functools
import math

import jax
import jax.numpy as jnp
import numpy as np
from jax import lax
from jax.experimental import pallas as pl
from jax.experimental.pallas import tpu as pltpu

F32 = jnp.float32
BF16 = jnp.bfloat16
HIGHEST = lax.Precision.HIGHEST

HEAD_DIM = 64
ATT_Q_HEADS = 8
ATT_KV_HEADS = 2
ATT_GROUP = ATT_Q_HEADS // ATT_KV_HEADS
ATT_W = ATT_Q_HEADS * HEAD_DIM
KV_W = ATT_KV_HEADS * HEAD_DIM
HY_W = 256
HY_SHORT = 3
HY_EMB = 33
HY_BANDS = (HY_EMB - 1) // 2
HY_FO = 64
HY_MIN_DECAY = math.log(1e-2) / 1.5
HY_MAX_DECAY = math.log(1e-2) / 0.3
NA_HEADS = 4
NA_W = NA_HEADS * HEAD_DIM
NA_WIN_ROWS = 8
NA_WIN_COLS = 16
D_MIX = ATT_W + HY_W + NA_W
D_IN = ATT_W + 2 * KV_W + 3 * HY_W + 3 * NA_W
TOP_K = 4
SWIGLU_ALPHA = 1.702
SWIGLU_LIMIT = 7.0
ROPE_THETA = 10000.0
NORM_EPS = 1e-6
Q_SCALE = HEAD_DIM ** -0.5

LANES = 128
SUBLANES = 8
NEG_BIG = -1e30

TOKEN_TILE = 256
ATT_Q_TILE = 128
NA_Q_ROWS = 8
NA_K_ROWS = 16
FFT_N1 = 8
HY_CB = 128
FREQ_CHUNK = 256
MOE_TILE = 256
VMEM_LIMIT = 56 * 1024 * 1024


def _cparams(sem, vmem=VMEM_LIMIT):
    return pltpu.CompilerParams(dimension_semantics=sem, vmem_limit_bytes=vmem)


def _layer_norm(x):
    mu = jnp.mean(x, axis=-1, keepdims=True)
    xc = x - mu
    var = jnp.mean(xc * xc, axis=-1, keepdims=True)
    return xc * lax.rsqrt(var + NORM_EPS)


def _dot(a, b):
    return jnp.dot(a, b, preferred_element_type=F32)


def _dot_nt(a, b):
    return lax.dot_general(a, b, (((1,), (1,)), ((), ())), preferred_element_type=F32)


def _split_bf16(x):
    hi = x.astype(BF16)
    lo = (x - hi.astype(F32)).astype(BF16)
    return hi, lo


def _dot3(a_hi, a_lo, b_hi, b_lo):
    return _dot(a_hi, b_hi) + _dot(a_hi, b_lo) + _dot(a_lo, b_hi)


def _adaln_kernel(c_ref, w_ref, b_ref, o_ref):
    c = c_ref[...]
    a = c * jax.nn.sigmoid(c)
    o_ref[0] = jnp.dot(a, w_ref[0], preferred_element_type=F32, precision=HIGHEST) + b_ref[0]


def _adaln(cvec, ada_w, ada_b):
    n_layers, d, d6 = ada_w.shape
    bn = d6 // 4 if (d6 // 4) % LANES == 0 else d6
    return pl.pallas_call(
        _adaln_kernel,
        grid=(n_layers, d6 // bn),
        in_specs=[
            pl.BlockSpec((SUBLANES, d), lambda l, j: (0, 0)),
            pl.BlockSpec((1, d, bn), lambda l, j: (l, 0, j)),
            pl.BlockSpec((1, 1, bn), lambda l, j: (l, 0, j)),
        ],
        out_specs=pl.BlockSpec((1, SUBLANES, bn), lambda l, j: (l, 0, j)),
        out_shape=jax.ShapeDtypeStruct((n_layers, SUBLANES, d6), F32),
        compiler_params=_cparams(("arbitrary", "arbitrary")),
        name="adaln",
    )(cvec, ada_w, ada_b.reshape(n_layers, 1, d6))


def _head_norm(a, g_ref, gain):
    w = a.shape[1]
    sq = a * a
    hi, lo = _split_bf16(sq)
    g = g_ref[0:w, 0:w]
    ss = _dot(hi, g) + _dot(lo, g)
    return a * lax.rsqrt(ss * (1.0 / HEAD_DIM) + NORM_EPS) * gain


def _rope(a, cos, sin_signed):
    w = a.shape[1]
    reps = w // LANES
    c = jnp.concatenate([cos] * reps, axis=1) if reps > 1 else cos
    s = jnp.concatenate([sin_signed] * reps, axis=1) if reps > 1 else sin_signed
    lane = lax.broadcasted_iota(jnp.int32, a.shape, 1)
    first = (lane & 31) < 16
    rot = jnp.where(first, pltpu.roll(a, w - 16, axis=1), pltpu.roll(a, 16, axis=1))
    return a * c + rot * s


def _proj_kernel(tmod_ref, tpos_ref, x_ref, mod_ref, w_ref, qg_ref, kg_ref, g_ref, cos_ref, sin_ref,
                 q_out, k_out, v_out, hy_out, nq_out, nk_out, nv_out):
    del tmod_ref, tpos_ref
    h = _layer_norm(x_ref[...]) * (1.0 + mod_ref[0, 1:2, :]) + mod_ref[0, 0:1, :]
    p = _dot(h.astype(BF16), w_ref[...])
    o = 0
    aq = p[:, o:o + ATT_W]; o += ATT_W
    ak = p[:, o:o + KV_W]; o += KV_W
    av = p[:, o:o + KV_W]; o += KV_W
    hy = p[:, o:o + 3 * HY_W]; o += 3 * HY_W
    nq = p[:, o:o + NA_W]; o += NA_W
    nk = p[:, o:o + NA_W]; o += NA_W
    nv = p[:, o:o + NA_W]
    cos = cos_ref[...]
    sin = sin_ref[...]
    q = _rope(_head_norm(aq, g_ref, qg_ref[...]), cos, sin) * Q_SCALE
    k = _rope(_head_norm(ak, g_ref, kg_ref[...]), cos, sin)
    q_out[...] = q.astype(BF16)
    k_out[...] = k.astype(BF16)
    v_out[...] = av.astype(BF16)
    hy_out[...] = hy
    nq_out[...] = (nq * Q_SCALE).astype(BF16)
    nk_out[...] = nk.astype(BF16)
    nv_out[...] = nv.astype(BF16)


def _proj(xf, mods_l, w_in_bf, q_gain, k_gain, gmat, cos_t, sin_t, tile_mod, tile_pos):
    t, d = xf.shape
    tm = TOKEN_TILE
    n_tiles = t // tm
    row = lambda i, tmod, tpos: (i, 0)
    const = lambda i, tmod, tpos: (0, 0)
    widths = (ATT_W, KV_W, KV_W, 3 * HY_W, NA_W, NA_W, NA_W)
    dtypes = (BF16, BF16, BF16, F32, BF16, BF16, BF16)
    grid_spec = pltpu.PrefetchScalarGridSpec(
        num_scalar_prefetch=2,
        grid=(n_tiles,),
        in_specs=[
            pl.BlockSpec((tm, d), row),
            pl.BlockSpec((1, 6, d), lambda i, tmod, tpos: (tmod[i], 0, 0)),
            pl.BlockSpec((d, D_IN), const),
            pl.BlockSpec((1, ATT_W), const),
            pl.BlockSpec((1, KV_W), const),
            pl.BlockSpec((ATT_W, ATT_W), const),
            pl.BlockSpec((tm, LANES), lambda i, tmod, tpos: (tpos[i], 0)),
            pl.BlockSpec((tm, LANES), lambda i, tmod, tpos: (tpos[i], 0)),
        ],
        out_specs=[pl.BlockSpec((tm, w), row) for w in widths],
    )
    return pl.pallas_call(
        _proj_kernel,
        grid_spec=grid_spec,
        out_shape=[jax.ShapeDtypeStruct((t, w), dt) for w, dt in zip(widths, dtypes)],
        compiler_params=_cparams(("arbitrary",)),
        name="ln_mod_proj",
    )(tile_mod, tile_pos, xf, mods_l, w_in_bf, q_gain, k_gain, gmat, cos_t, sin_t)


def _attend(q, segs, n_kv, grp, bias=None):
    tq = q.shape[0]
    hd = HEAD_DIM
    outs = []
    for j in range(n_kv):
        heads = [q[:, (j * grp + g) * hd:(j * grp + g + 1) * hd] for g in range(grp)]
        qs = jnp.concatenate(heads, axis=0) if grp > 1 else heads[0]
        scores = []
        for si, (k, _) in enumerate(segs):
            s = _dot_nt(qs, k[:, j * hd:(j + 1) * hd])
            if bias is not None and bias[j][si] is not None:
                s = s + bias[j][si]
            scores.append(s)
        m = scores[0].max(axis=-1, keepdims=True)
        for s in scores[1:]:
            m = jnp.maximum(m, s.max(axis=-1, keepdims=True))
        denom = None
        acc = None
        for s, (_, v) in zip(scores, segs):
            p = jnp.exp(s - m)
            ps = p.sum(axis=-1, keepdims=True)
            pv = _dot(p.astype(BF16), v[:, j * hd:(j + 1) * hd])
            denom = ps if denom is None else denom + ps
            acc = pv if acc is None else acc + pv
        o = acc / denom
        for g in range(grp):
            outs.append(o[g * tq:(g + 1) * tq])
    return jnp.concatenate(outs, axis=1)


def _gqa_kernel(q_ref, kl_ref, vl_ref, kc_ref, vc_ref, o_ref):
    segs = [(kl_ref[...], vl_ref[...]), (kc_ref[...], vc_ref[...])]
    o_ref[...] = _attend(q_ref[...], segs, ATT_KV_HEADS, ATT_GROUP).astype(o_ref.dtype)


def _gqa(q, k, v, nb, n, m, with_ctx):
    t = q.shape[0]
    tq = ATT_Q_TILE
    n_lat = n // tq
    ctx_blk0 = nb * n // m
    qmap = lambda b, i: (b * n_lat + i, 0)
    lat = lambda b, i: (b, 0)
    ctx = lambda b, i: (ctx_blk0 + b, 0)
    y = pl.pallas_call(
        _gqa_kernel,
        grid=(nb, n_lat),
        in_specs=[
            pl.BlockSpec((tq, ATT_W), qmap),
            pl.BlockSpec((n, KV_W), lat),
            pl.BlockSpec((n, KV_W), lat),
            pl.BlockSpec((m, KV_W), ctx),
            pl.BlockSpec((m, KV_W), ctx),
        ],
        out_specs=pl.BlockSpec((tq, ATT_W), qmap),
        out_shape=jax.ShapeDtypeStruct((nb * n, ATT_W), BF16),
        compiler_params=_cparams(("arbitrary", "arbitrary")),
        name="gqa_attention",
    )(q, k, v, k, v)
    if not with_ctx:
        return y
    yc = _ctx_attention(q, k, v, nb, n, m, ATT_KV_HEADS, ATT_GROUP, ATT_W, "gqa_ctx_attention")
    return jnp.concatenate([y, yc], axis=0)


def _na_kernel(q_ref, kl_ref, vl_ref, kc_ref, vc_ref, bias_ref, o_ref, *, grid_w, grid_rows):
    i = pl.program_id(1)
    krow0 = jnp.clip(i * NA_Q_ROWS - NA_WIN_ROWS // 2, 0, grid_rows - NA_K_ROWS)
    start = pl.multiple_of(krow0 * grid_w, 4 * grid_w)
    nk = NA_K_ROWS * grid_w
    segs = [(kl_ref[pl.ds(start, nk), :], vl_ref[pl.ds(start, nk), :]), (kc_ref[...], vc_ref[...])]
    bias = [[bias_ref[0, h], None] for h in range(NA_HEADS)]
    o_ref[...] = _attend(q_ref[...], segs, NA_HEADS, 1, bias).astype(o_ref.dtype)


def _na_bias_table(rpb, grid_w, grid_rows):
    qr, kr = NA_Q_ROWS, NA_K_ROWS
    tabs = []
    for r0 in (0, qr, grid_rows - qr):
        k0 = int(np.clip(r0 - NA_WIN_ROWS // 2, 0, grid_rows - kr))
        r = r0 + np.arange(qr)[:, None, None, None]
        c = np.arange(grid_w)[None, :, None, None]
        krow = k0 + np.arange(kr)[None, None, :, None]
        kcol = np.arange(grid_w)[None, None, None, :]
        rs = np.clip(r - NA_WIN_ROWS // 2, 0, grid_rows - NA_WIN_ROWS)
        cs = np.clip(c - NA_WIN_COLS // 2, 0, grid_w - NA_WIN_COLS)
        valid = (krow >= rs) & (krow < rs + NA_WIN_ROWS) & (kcol >= cs) & (kcol < cs + NA_WIN_COLS)
        rel_r = np.clip(krow - r + NA_WIN_ROWS - 1, 0, 2 * NA_WIN_ROWS - 2)
        rel_c = np.clip(kcol - c + NA_WIN_COLS - 1, 0, 2 * NA_WIN_COLS - 2)
        shape = (qr, grid_w, kr, grid_w)
        flat = (np.broadcast_to(rel_r, shape) * (2 * NA_WIN_COLS - 1) + np.broadcast_to(rel_c, shape)).reshape(-1)
        vals = jnp.take(rpb.reshape(NA_HEADS, -1), jnp.asarray(flat, jnp.int32), axis=1)
        vals = jnp.where(jnp.asarray(np.broadcast_to(valid, shape).reshape(-1)), vals, NEG_BIG)
        tabs.append(vals.reshape(NA_HEADS, qr * grid_w, kr * grid_w))
    return jnp.stack(tabs, axis=0)


def _na(nq, nk, nv, bias_tab, nb, n, m, grid_w, with_ctx):
    t = nq.shape[0]
    grid_rows = n // grid_w
    tq = NA_Q_ROWS * grid_w
    n_lat = n // tq
    ctx_blk0 = nb * n // m
    lat = lambda b, i: (b, 0)
    ctx = lambda b, i: (ctx_blk0 + b, 0)
    qmap = lambda b, i: (b * n_lat + i, 0)

    def bmap(b, i):
        return (jnp.where(i == 0, 0, jnp.where(i >= n_lat - 1, 2, 1)), 0, 0, 0)

    y = pl.pallas_call(
        functools.partial(_na_kernel, grid_w=grid_w, grid_rows=grid_rows),
        grid=(nb, n_lat),
        in_specs=[
            pl.BlockSpec((tq, NA_W), qmap),
            pl.BlockSpec((n, NA_W), lat),
            pl.BlockSpec((n, NA_W), lat),
            pl.BlockSpec((m, NA_W), ctx),
            pl.BlockSpec((m, NA_W), ctx),
            pl.BlockSpec((1, NA_HEADS, tq, NA_K_ROWS * grid_w), bmap),
        ],
        out_specs=pl.BlockSpec((tq, NA_W), qmap),
        out_shape=jax.ShapeDtypeStruct((nb * n, NA_W), BF16),
        compiler_params=_cparams(("arbitrary", "arbitrary")),
        name="neighbourhood_attention",
    )(nq, nk, nv, nk, nv, bias_tab)
    if not with_ctx:
        return y
    yc = _ctx_attention(nq, nk, nv, nb, n, m, NA_HEADS, 1, NA_W, "na_ctx_attention")
    return jnp.concatenate([y, yc], axis=0)


def _ctx_attn_kernel(q_ref, k_ref, v_ref, o_ref, *, n_kv, grp):
    segs = [(k_ref[...], v_ref[...])]
    o_ref[...] = _attend(q_ref[...], segs, n_kv, grp).astype(o_ref.dtype)


def _ctx_attention(q, k, v, nb, n, m, n_kv, grp, width, name):
    ctx_blk0 = nb * n // m
    ctx = lambda b: (ctx_blk0 + b, 0)
    kw = n_kv * HEAD_DIM
    return pl.pallas_call(
        functools.partial(_ctx_attn_kernel, n_kv=n_kv, grp=grp),
        grid=(nb,),
        in_specs=[
            pl.BlockSpec((m, width), ctx),
            pl.BlockSpec((m, kw), ctx),
            pl.BlockSpec((m, kw), ctx),
        ],
        out_specs=pl.BlockSpec((m, width), lambda b: (b, 0)),
        out_shape=jax.ShapeDtypeStruct((nb * m, width), BF16),
        compiler_params=_cparams(("arbitrary",)),
        name=name,
    )(q, k, v)


def _dft_constants(seq_len):
    n = 2 * seq_len
    n2 = n // FFT_N1
    idx = np.arange(n2, dtype=np.int64)
    ang = 2.0 * np.pi * ((idx[:, None] * idx[None, :]) % n2).astype(np.float64) / n2
    cmat, smat = np.cos(ang), np.sin(ang)

    def split(a):
        hi = jnp.asarray(a, F32).astype(BF16)
        lo = (jnp.asarray(a, F32) - hi.astype(F32)).astype(BF16)
        return hi, lo

    def col_chunks(a):
        return np.transpose(a.reshape(a.shape[0], n2 // chunk, chunk), (1, 0, 2))

    chunk = min(FREQ_CHUNK, n2)
    tw_ang = 2.0 * np.pi * (idx[:, None] * np.arange(FFT_N1)[None, :]).astype(np.float64) / n
    return dict(
        n2=n2, chunk=chunk,
        fwd_c=split(cmat), fwd_s=split(smat),
        inv_c=split(col_chunks(cmat[: n2 // 2] / n)),
        inv_s=split(col_chunks(smat[: n2 // 2] / n)),
        tw_re=jnp.asarray(np.cos(tw_ang), F32), tw_im=jnp.asarray(-np.sin(tw_ang), F32),
    )


def _cmul_const(a, w):
    re, im = a
    if abs(w - 1) < 1e-12:
        return a
    if abs(w + 1) < 1e-12:
        return (-re, -im)
    if abs(w + 1j) < 1e-12:
        return (im, -re)
    if abs(w - 1j) < 1e-12:
        return (-im, re)
    return (re * w.real - im * w.imag, re * w.imag + im * w.real)


def _fft_blocks(xs, sign):
    n = len(xs)
    if n == 1:
        return xs
    even = _fft_blocks(xs[0::2], sign)
    odd = _fft_blocks(xs[1::2], sign)
    out = [None] * n
    for k in range(n // 2):
        t = _cmul_const(odd[k], cmath.exp(sign * 2j * cmath.pi * k / n))
        out[k] = (even[k][0] + t[0], even[k][1] + t[1])
        out[k + n // 2] = (even[k][0] - t[0], even[k][1] - t[1])
    return out


def _lane_blocks(a):
    return [a[:, j * LANES:(j + 1) * LANES] for j in range(a.shape[1] // LANES)]


def _forward_spectrum(xh, xl, rows, ch_ref, cl_ref, sh_ref, sl_ref, twr_ref, twi_ref):
    a_re = _dot3(ch_ref[rows, :], cl_ref[rows, :], xh, xl)
    a_im = -_dot3(sh_ref[rows, :], sl_ref[rows, :], xh, xl)
    blocks = []
    for s1, (re, im) in enumerate(zip(_lane_blocks(a_re), _lane_blocks(a_im))):
        tr = twr_ref[rows, s1:s1 + 1]
        ti = twi_ref[rows, s1:s1 + 1]
        blocks.append((re * tr - im * ti, re * ti + im * tr))
    return _fft_blocks(blocks, -1)


def _filter_kernel(emb_ref, w1_ref, b1_ref, fr_ref, w2_ref, b2_ref, w3f_ref, w3b_ref, b3f_ref, b3b_ref,
                   dec_ref, ch_ref, cl_ref, sh_ref, sl_ref, twr_ref, twi_ref, gre_ref, gim_ref, g_scr, *,
                   seq_len, chunk):
    n = 2 * seq_len
    n2 = n // FFT_N1
    emb = emb_ref[...]
    hid = jnp.sin(fr_ref[0:1, :] * (jnp.dot(emb, w1_ref[...], preferred_element_type=F32, precision=HIGHEST)
                                    + b1_ref[...]))
    hid = jnp.sin(fr_ref[1:2, :] * (jnp.dot(hid, w2_ref[...], preferred_element_type=F32, precision=HIGHEST)
                                    + b2_ref[...]))
    f_fwd = jnp.dot(hid, w3f_ref[...], preferred_element_type=F32, precision=HIGHEST) + b3f_ref[...]
    f_bwd = jnp.dot(hid, w3b_ref[...], preferred_element_type=F32, precision=HIGHEST) + b3b_ref[...]
    row = lax.broadcasted_iota(jnp.int32, f_fwd.shape, 0)
    t_pos = emb[:, 0:1]
    g = jnp.where(row < seq_len, f_fwd, jnp.where(row > seq_len, f_bwd, 0.0))
    g = g * jnp.exp(-t_pos * jnp.abs(dec_ref[...]))
    g = g * lax.rsqrt(jnp.sum(g * g, axis=0, keepdims=True) + NORM_EPS)
    g_scr[...] = g
    x = jnp.concatenate([g_scr[pl.ds(s1, n2, stride=FFT_N1), :] for s1 in range(FFT_N1)], axis=1)
    xh, xl = _split_bf16(x)

    def freq_chunk(ci, carry):
        rows = pl.ds(pl.multiple_of(ci * chunk, chunk), chunk)
        spec = _forward_spectrum(xh, xl, rows, ch_ref, cl_ref, sh_ref, sl_ref, twr_ref, twi_ref)
        gre_ref[0, rows, :] = jnp.concatenate([b[0] for b in spec], axis=1)
        gim_ref[0, rows, :] = jnp.concatenate([b[1] for b in spec], axis=1)
        return carry

    lax.fori_loop(0, n2 // chunk, freq_chunk, 0)


def _filter_embedding(seq_len):
    n = 2 * seq_len
    pos = np.concatenate([np.arange(seq_len), [0], np.arange(seq_len - 1, 0, -1)])
    t = np.linspace(0.0, 1.0, seq_len, dtype=np.float32)[pos][:, None]
    w = ((2.0 * math.pi / seq_len) * np.arange(seq_len, dtype=np.float32))[pos][:, None]
    bands = np.linspace(1e-4, HY_BANDS - 1, HY_BANDS, dtype=np.float32)[None, :]
    z = np.concatenate([t, np.cos(bands * w), -np.sin(bands * w)], axis=-1).astype(np.float32)
    emb = np.zeros((n, LANES), np.float32)
    emb[:, :HY_EMB] = z
    return jnp.asarray(emb)


def _hyena_filter_spectrum(seq_len, consts, f_w1, f_b1, f_freq, f_w2, f_b2, f_w3, f_b3):
    n = 2 * seq_len
    n2 = consts["n2"]
    ncb = HY_W // HY_CB
    emb = _filter_embedding(seq_len)
    w1p = jnp.zeros((LANES, HY_FO), F32).at[:HY_EMB].set(f_w1)
    deltas = jnp.asarray(np.linspace(HY_MIN_DECAY, HY_MAX_DECAY, HY_W, dtype=np.float32))[None, :]
    const = lambda j: (0, 0)
    out_sds = jax.ShapeDtypeStruct((ncb, n2, FFT_N1 * HY_CB), F32)
    return pl.pallas_call(
        functools.partial(_filter_kernel, seq_len=seq_len, chunk=consts["chunk"]),
        grid=(ncb,),
        in_specs=[
            pl.BlockSpec((n, LANES), const),
            pl.BlockSpec((LANES, HY_FO), const),
            pl.BlockSpec((1, HY_FO), const),
            pl.BlockSpec((2, HY_FO), const),
            pl.BlockSpec((HY_FO, HY_FO), const),
            pl.BlockSpec((1, HY_FO), const),
            pl.BlockSpec((HY_FO, HY_CB), lambda j: (0, j)),
            pl.BlockSpec((HY_FO, HY_CB), lambda j: (0, ncb + j)),
            pl.BlockSpec((1, HY_CB), lambda j: (0, j)),
            pl.BlockSpec((1, HY_CB), lambda j: (0, ncb + j)),
            pl.BlockSpec((1, HY_CB), lambda j: (0, j)),
        ] + [pl.BlockSpec((n2, n2), const, pipeline_mode=pl.Buffered(1))] * 4
          + [pl.BlockSpec((n2, FFT_N1), const, pipeline_mode=pl.Buffered(1))] * 2,
        out_specs=[pl.BlockSpec((1, n2, FFT_N1 * HY_CB), lambda j: (j, 0, 0))] * 2,
        out_shape=[out_sds, out_sds],
        scratch_shapes=[pltpu.VMEM((n, HY_CB), F32)],
        compiler_params=_cparams(("arbitrary",)),
        name=f"hyena_filter_{seq_len}",
    )(emb, w1p, f_b1[None, :], f_freq, f_w2, f_b2[None, :], f_w3, f_w3, f_b3[None, :], f_b3[None, :], deltas,
      *consts["fwd_c"], *consts["fwd_s"], consts["tw_re"], consts["tw_im"])


def _short_conv(p, w_ref, b_ref):
    n = p.shape[0]
    row = lax.broadcasted_iota(jnp.int32, p.shape, 0)
    prev = jnp.where(row == 0, 0.0, pltpu.roll(p, 1, axis=0))
    nxt = jnp.where(row == n - 1, 0.0, pltpu.roll(p, n - 1, axis=0))
    return prev * w_ref[0:1, :] + p * w_ref[1:2, :] + nxt * w_ref[2:3, :] + b_ref[...]


def _hyena_kernel(p0_ref, p1_ref, p2_ref, w0_ref, w1_ref, w2_ref, b0_ref, b1_ref, b2_ref, d_ref,
                  gre_ref, gim_ref, fch_ref, fcl_ref, fsh_ref, fsl_ref, ich_ref, icl_ref, ish_ref, isl_ref,
                  twr_ref, twi_ref, o_ref, z_scr, y_scr, acc_scr, *, seq_len, chunk):
    n2 = 2 * seq_len // FFT_N1
    half = n2 // 2
    x1 = _short_conv(p1_ref[...], w1_ref, b1_ref)
    v = _short_conv(p2_ref[...], w2_ref, b2_ref)
    z_scr[...] = x1 * v
    x = jnp.concatenate([z_scr[pl.ds(s1, half, stride=FFT_N1), :] for s1 in range(FFT_N1)], axis=1)
    xh, xl = _split_bf16(x)
    acc_scr[...] = jnp.zeros(acc_scr.shape, F32)

    def freq_chunk(ci, carry):
        rows = pl.ds(pl.multiple_of(ci * chunk, chunk), chunk)
        zf = _forward_spectrum(xh, xl, rows, fch_ref, fcl_ref, fsh_ref, fsl_ref, twr_ref, twi_ref)
        gre = _lane_blocks(gre_ref[0, rows, :])
        gim = _lane_blocks(gim_ref[0, rows, :])
        prod = [(zr * gr - zi * gi, zr * gi + zi * gr) for (zr, zi), gr, gi in zip(zf, gre, gim)]
        back = _fft_blocks(prod, +1)
        c_re, c_im = [], []
        for s1, (re, im) in enumerate(back):
            tr = twr_ref[rows, s1:s1 + 1]
            ti = -twi_ref[rows, s1:s1 + 1]
            c_re.append(re * tr - im * ti)
            c_im.append(re * ti + im * tr)
        rh, rl = _split_bf16(jnp.concatenate(c_re, axis=1))
        ih, il = _split_bf16(jnp.concatenate(c_im, axis=1))
        acc_scr[...] += _dot3(ich_ref[ci], icl_ref[ci], rh, rl) - _dot3(ish_ref[ci], isl_ref[ci], ih, il)
        return carry

    lax.fori_loop(0, n2 // chunk, freq_chunk, 0)
    for s1 in range(FFT_N1):
        y_scr[pl.ds(s1, half, stride=FFT_N1), :] = acc_scr[:, s1 * LANES:(s1 + 1) * LANES]
    x0 = _short_conv(p0_ref[...], w0_ref, b0_ref)
    o_ref[...] = (x0 * (y_scr[...] + z_scr[...] * d_ref[...])).astype(o_ref.dtype)


def _hyena(hyp, conv_w, conv_b, d_skip, gre, gim, consts, nb, seq_len, row_blk0):
    n2 = consts["n2"]
    chunk = consts["chunk"]
    ncb = HY_W // HY_CB
    half = n2 // 2
    const = lambda j, b: (0, 0)
    once = pl.Buffered(1)
    in_specs = (
        [pl.BlockSpec((seq_len, HY_CB), (lambda j, b, g=g: (row_blk0 + b, g * ncb + j))) for g in range(3)]
        + [pl.BlockSpec((HY_SHORT, HY_CB), (lambda j, b, g=g: (0, g * ncb + j))) for g in range(3)]
        + [pl.BlockSpec((1, HY_CB), (lambda j, b, g=g: (0, g * ncb + j))) for g in range(3)]
        + [pl.BlockSpec((1, HY_CB), lambda j, b: (0, j))]
        + [pl.BlockSpec((1, n2, FFT_N1 * HY_CB), lambda j, b: (j, 0, 0), pipeline_mode=once)] * 2
        + [pl.BlockSpec((n2, half), const, pipeline_mode=once)] * 4
        + [pl.BlockSpec((n2 // chunk, half, chunk), lambda j, b: (0, 0, 0), pipeline_mode=once)] * 4
        + [pl.BlockSpec((n2, FFT_N1), const, pipeline_mode=once)] * 2
    )
    fwd = [a[:, :half] for a in (*consts["fwd_c"], *consts["fwd_s"])]
    args = [hyp, hyp, hyp, conv_w, conv_w, conv_w, conv_b, conv_b, conv_b, d_skip, gre, gim,
            *fwd, *consts["inv_c"], *consts["inv_s"], consts["tw_re"], consts["tw_im"]]
    return pl.pallas_call(
        functools.partial(_hyena_kernel, seq_len=seq_len, chunk=chunk),
        grid=(ncb, nb),
        in_specs=in_specs,
        out_specs=pl.BlockSpec((seq_len, HY_CB), lambda j, b: (b, j)),
        out_shape=jax.ShapeDtypeStruct((nb * seq_len, HY_W), BF16),
        scratch_shapes=[pltpu.VMEM((seq_len, HY_CB), F32), pltpu.VMEM((seq_len, HY_CB), F32),
                        pltpu.VMEM((half, FFT_N1 * HY_CB), F32)],
        compiler_params=_cparams(("arbitrary", "arbitrary")),
        name=f"hyena_mixer_{seq_len}",
    )(*args)


def _pack_pairs(h):
    half = h.shape[1] // 2
    lo = pltpu.bitcast(h[:, :half].astype(BF16).astype(F32), jnp.uint32) >> 16
    hi = pltpu.bitcast(h[:, half:].astype(BF16).astype(F32), jnp.uint32) & jnp.uint32(0xFFFF0000)
    return lo | hi


def _unpack_pairs(u):
    lo = pltpu.bitcast(u << 16, F32)
    hi = pltpu.bitcast(u & jnp.uint32(0xFFFF0000), F32)
    return jnp.concatenate([lo, hi], axis=1).astype(BF16)


def _mix_out_kernel(tmod_ref, x_ref, ya_ref, yh_ref, yn_ref, mod_ref, w_ref, g_ref, b_ref, rw_ref, rb_ref,
                    x1_out, h2_out, idx_out, wt_out, *, alpha, n_experts):
    del tmod_ref
    y = _dot(jnp.concatenate([ya_ref[...], yh_ref[...], yn_ref[...]], axis=1), w_ref[...])
    x1 = _layer_norm(alpha * x_ref[...] + mod_ref[0, 2:3, :] * y) * g_ref[...] + b_ref[...]
    x1_out[...] = x1
    h2 = _layer_norm(x1) * (1.0 + mod_ref[0, 4:5, :]) + mod_ref[0, 3:4, :]
    h2_out[...] = _pack_pairs(h2)
    logits = jnp.dot(h2, rw_ref[...], preferred_element_type=F32, precision=HIGHEST) + rb_ref[...]
    lane = lax.broadcasted_iota(jnp.int32, logits.shape, 1)
    lane_f = lane.astype(F32)
    cur = jnp.where(lane < n_experts, logits, -jnp.inf)
    vals, idxs = [], []
    for _ in range(TOP_K):
        m = cur.max(axis=-1, keepdims=True)
        ix = jnp.where(cur == m, lane_f, float(LANES)).min(axis=-1, keepdims=True)
        vals.append(m)
        idxs.append(ix)
        cur = jnp.where(lane_f == ix, -jnp.inf, cur)
    es = [jnp.exp(v - vals[0]) for v in vals]
    denom = es[0]
    for e in es[1:]:
        denom = denom + e
    idx_v = jnp.zeros(logits.shape, F32)
    wt_v = jnp.zeros(logits.shape, F32)
    for k in range(TOP_K):
        idx_v = jnp.where(lane == k, idxs[k], idx_v)
        wt_v = jnp.where(lane == k, es[k] / denom, wt_v)
    idx_out[...] = idx_v.astype(jnp.int32)
    wt_out[...] = wt_v


def _mix_out(xf, y_att, y_hy, y_na, mods_l, w_out_bf, ln_g, ln_b, router_wp, router_bp, tile_mod, n_tok,
             alpha, n_experts):
    d = xf.shape[1]
    tm = TOKEN_TILE
    row = lambda i, tmod: (i, 0)
    const = lambda i, tmod: (0, 0)
    grid_spec = pltpu.PrefetchScalarGridSpec(
        num_scalar_prefetch=1,
        grid=(n_tok // tm,),
        in_specs=[
            pl.BlockSpec((tm, d), row),
            pl.BlockSpec((tm, ATT_W), row),
            pl.BlockSpec((tm, HY_W), row),
            pl.BlockSpec((tm, NA_W), row),
            pl.BlockSpec((1, 6, d), lambda i, tmod: (tmod[i], 0, 0)),
            pl.BlockSpec((D_MIX, d), const),
            pl.BlockSpec((1, d), const),
            pl.BlockSpec((1, d), const),
            pl.BlockSpec((d, LANES), const),
            pl.BlockSpec((1, LANES), const),
        ],
        out_specs=[
            pl.BlockSpec((tm, d), row),
            pl.BlockSpec((tm, d // 2), row),
            pl.BlockSpec((tm, LANES), row),
            pl.BlockSpec((tm, LANES), row),
        ],
    )
    return pl.pallas_call(
        functools.partial(_mix_out_kernel, alpha=alpha, n_experts=n_experts),
        grid_spec=grid_spec,
        out_shape=[
            jax.ShapeDtypeStruct((n_tok, d), F32),
            jax.ShapeDtypeStruct((n_tok, d // 2), jnp.uint32),
            jax.ShapeDtypeStruct((n_tok, LANES), jnp.int32),
            jax.ShapeDtypeStruct((n_tok, LANES), F32),
        ],
        compiler_params=_cparams(("arbitrary",)),
        name="mix_out_norm_route",
    )(tile_mod, xf, y_att, y_hy, y_na, mods_l, w_out_bf, ln_g, ln_b, router_wp, router_bp)


def _dispatch_kernel(slot_ref, h_hbm, xs_in, xs_hbm, sem, *, tm):
    del xs_in
    base = pl.program_id(0) * tm

    def row_copy(src_row, dst_row):
        return pltpu.make_async_copy(h_hbm.at[pl.ds(src_row, 1)], xs_hbm.at[pl.ds(dst_row, 1)], sem)

    def issue(t, carry):
        for k in range(TOP_K):
            row_copy(base + t, slot_ref[t * TOP_K + k]).start()
        return carry

    lax.fori_loop(0, tm, issue, 0, unroll=8)

    def drain(t, carry):
        for k in range(TOP_K):
            row_copy(0, 0).wait()
        return carry

    lax.fori_loop(0, tm, drain, 0, unroll=8)


def _dispatch(h2p, slots_flat, n_slots):
    n_tok, hw = h2p.shape
    tm = TOKEN_TILE
    xs0 = jnp.zeros((n_slots, hw), h2p.dtype)
    return pl.pallas_call(
        functools.partial(_dispatch_kernel, tm=tm),
        grid=(n_tok // tm,),
        in_specs=[
            pl.BlockSpec((tm * TOP_K,), lambda i: (i,), memory_space=pltpu.SMEM),
            pl.BlockSpec(memory_space=pl.ANY),
            pl.BlockSpec(memory_space=pl.ANY),
        ],
        out_specs=pl.BlockSpec(memory_space=pl.ANY),
        out_shape=jax.ShapeDtypeStruct((n_slots, hw), h2p.dtype),
        scratch_shapes=[pltpu.SemaphoreType.DMA],
        input_output_aliases={2: 0},
        compiler_params=_cparams(("arbitrary",)),
        name="moe_dispatch",
    )(slots_flat, h2p, xs0)


def _expert_kernel(te_ref, nu_ref, x_ref, w1_ref, b1_ref, w2_ref, b2_ref, o_ref, *, d_expert):
    i = pl.program_id(0)

    @pl.when(i < nu_ref[0])
    def _():
        x = _unpack_pairs(x_ref[...])
        hid = _dot(x, w1_ref[0]) + b1_ref[0]
        glu = jnp.minimum(hid[:, :d_expert], SWIGLU_LIMIT)
        lin = jnp.clip(hid[:, d_expert:], -SWIGLU_LIMIT, SWIGLU_LIMIT)
        act = glu * jax.nn.sigmoid(SWIGLU_ALPHA * glu) * (lin + 1.0)
        o_ref[...] = _dot(act.astype(BF16), w2_ref[0]) + b2_ref[0]

    @pl.when(i >= nu_ref[0])
    def _():
        o_ref[...] = jnp.zeros(o_ref.shape, o_ref.dtype)


def _experts(xs, tile_expert, n_used, w1_bf, b1, w2_bf, b2):
    n_slots, hw = xs.shape
    n_exp, d, d2 = w1_bf.shape
    d_expert = d2 // 2
    tm = MOE_TILE
    n_tiles = n_slots // tm
    grid_spec = pltpu.PrefetchScalarGridSpec(
        num_scalar_prefetch=2,
        grid=(n_tiles,),
        in_specs=[
            pl.BlockSpec((tm, hw), lambda i, te, nu: (i, 0)),
            pl.BlockSpec((1, d, d2), lambda i, te, nu: (te[i], 0, 0)),
            pl.BlockSpec((1, 1, d2), lambda i, te, nu: (te[i], 0, 0)),
            pl.BlockSpec((1, d_expert, d), lambda i, te, nu: (te[i], 0, 0)),
            pl.BlockSpec((1, 1, d), lambda i, te, nu: (te[i], 0, 0)),
        ],
        out_specs=pl.BlockSpec((tm, d), lambda i, te, nu: (i, 0)),
    )
    return pl.pallas_call(
        functools.partial(_expert_kernel, d_expert=d_expert),
        grid_spec=grid_spec,
        out_shape=jax.ShapeDtypeStruct((n_slots, d), F32),
        compiler_params=_cparams(("arbitrary",)),
        name="moe_experts",
    )(tile_expert, n_used, xs, w1_bf, b1.reshape(n_exp, 1, d2), w2_bf, b2.reshape(n_exp, 1, d))


def _combine_kernel(tmod_ref, slot_ref, x1_ref, wt_ref, mod_ref, g_ref, b_ref, ys_hbm, o_ref, buf, sem, *, tm, alpha):
    del tmod_ref

    def row_copy(src_row, k, t):
        return pltpu.make_async_copy(ys_hbm.at[pl.ds(src_row, 1)], buf.at[k, pl.ds(t, 1)], sem)

    def issue(t, carry):
        for k in range(TOP_K):
            row_copy(slot_ref[t * TOP_K + k], k, t).start()
        return carry

    lax.fori_loop(0, tm, issue, 0, unroll=8)

    def drain(t, carry):
        for k in range(TOP_K):
            row_copy(0, k, t).wait()
        return carry

    lax.fori_loop(0, tm, drain, 0, unroll=8)
    wt = wt_ref[...]
    moe = buf[0] * wt[:, 0:1]
    for k in range(1, TOP_K):
        moe = moe + buf[k] * wt[:, k:k + 1]
    o_ref[...] = _layer_norm(alpha * x1_ref[...] + mod_ref[0, 5:6, :] * moe) * g_ref[...] + b_ref[...]


def _combine(x1, wts, slots_flat, ys, mods_l, ln_g, ln_b, tile_mod, alpha):
    n_tok, d = x1.shape
    tm = TOKEN_TILE
    row = lambda i, tmod: (i, 0)
    const = lambda i, tmod: (0, 0)
    grid_spec = pltpu.PrefetchScalarGridSpec(
        num_scalar_prefetch=1,
        grid=(n_tok // tm,),
        in_specs=[
            pl.BlockSpec((tm * TOP_K,), lambda i, tmod: (i,), memory_space=pltpu.SMEM),
            pl.BlockSpec((tm, d), row),
            pl.BlockSpec((tm, LANES), row),
            pl.BlockSpec((1, 6, d), lambda i, tmod: (tmod[i], 0, 0)),
            pl.BlockSpec((1, d), const),
            pl.BlockSpec((1, d), const),
            pl.BlockSpec(memory_space=pl.ANY),
        ],
        out_specs=pl.BlockSpec((tm, d), row),
        scratch_shapes=[pltpu.VMEM((TOP_K, tm, d), F32), pltpu.SemaphoreType.DMA],
    )
    return pl.pallas_call(
        functools.partial(_combine_kernel, tm=tm, alpha=alpha),
        grid_spec=grid_spec,
        out_shape=jax.ShapeDtypeStruct((n_tok, d), F32),
        compiler_params=_cparams(("arbitrary",)),
        name="moe_combine_norm",
    )(tile_mod, slots_flat, x1, wts, mods_l, ln_g, ln_b, ys)


def _routing_tables(idx4, n_experts, tile):
    n_tok = idx4.shape[0]
    onehot = (idx4[:, :, None] == jnp.arange(n_experts, dtype=jnp.int32)[None, None, :]).astype(jnp.int32).sum(axis=1)
    counts = onehot.sum(axis=0)
    pos = jnp.cumsum(onehot, axis=0) - onehot
    padded = ((counts + tile - 1) // tile) * tile
    ends = jnp.cumsum(padded)
    offs = ends - padded
    slots = offs[idx4] + jnp.take_along_axis(pos, idx4, axis=1)
    n_tiles = (n_tok * TOP_K) // tile + n_experts
    tile_start = jnp.arange(n_tiles, dtype=jnp.int32) * tile
    tile_expert = jnp.minimum(jnp.searchsorted(ends, tile_start, side="right"), n_experts - 1).astype(jnp.int32)
    n_used = (ends[-1] // tile).astype(jnp.int32)
    last = tile_expert[jnp.maximum(n_used - 1, 0)]
    tile_expert = jnp.where(jnp.arange(n_tiles) < n_used, tile_expert, last)
    return slots.astype(jnp.int32).reshape(-1), tile_expert, n_used.reshape(1), n_tiles * tile


def _rope_tables(n, grid_w, extra_rows):
    t = np.arange(n)
    row = (t // grid_w).astype(np.float32)[:, None]
    col = (t % grid_w).astype(np.float32)[:, None]
    axis_dim = HEAD_DIM // 2
    inv_freq = (ROPE_THETA ** (-np.arange(0, axis_dim, 2, dtype=np.float32) / axis_dim)).astype(np.float32)
    ang_r = row * inv_freq
    ang_c = col * inv_freq
    ang = np.concatenate([ang_r, ang_r, ang_c, ang_c], axis=-1)
    cos = np.cos(ang).astype(np.float32)
    sin = np.sin(ang).astype(np.float32)
    sign = np.where((np.arange(HEAD_DIM) % 32) < 16, -1.0, 1.0).astype(np.float32)
    sin = sin * sign[None, :]
    cos = np.concatenate([cos, np.ones((extra_rows, HEAD_DIM), np.float32)], axis=0)
    sin = np.concatenate([sin, np.zeros((extra_rows, HEAD_DIM), np.float32)], axis=0)
    reps = LANES // HEAD_DIM
    return jnp.asarray(np.tile(cos, (1, reps))), jnp.asarray(np.tile(sin, (1, reps)))


def _forward(x, c, ctx, c_ctx, ada_w, ada_b, w_in, w_out, q_gain, k_gain, hy_conv_w, hy_conv_b,
             hy_w1, hy_b1, hy_freq, hy_w2, hy_b2, hy_w3, hy_b3, hy_d, na_rpb, ln1_g, ln1_b,
             ln2_g, ln2_b, router_w, router_b, exp_w1, exp_b1, exp_w2, exp_b2, *, grid_w):
    nb, n, d = x.shape
    m = ctx.shape[1]
    depth = ada_w.shape[0]
    n_experts = router_w.shape[-1]
    alpha = (2.0 * depth) ** 0.25
    tm = TOKEN_TILE
    assert n % tm == 0 and m % tm == 0 and n % m == 0 and n_experts <= LANES
    t_lat, t_all = nb * n, nb * (n + m)

    xf = jnp.concatenate([x.reshape(nb * n, d), ctx.reshape(nb * m, d)], axis=0)
    cvec = jnp.zeros((SUBLANES, d), F32).at[:nb].set(c).at[nb].set(c_ctx)
    mods = _adaln(cvec, ada_w, ada_b).reshape(depth, SUBLANES, 6, d)

    tiles = np.arange(t_all // tm)
    lat_tile = tiles < t_lat // tm
    tile_mod = jnp.asarray(np.where(lat_tile, tiles // (n // tm), nb), jnp.int32)
    tile_pos = jnp.asarray(np.where(lat_tile, tiles % (n // tm), n // tm), jnp.int32)
    cos_t, sin_t = _rope_tables(n, grid_w, tm)
    gmat = jnp.asarray(np.kron(np.eye(ATT_Q_HEADS), np.ones((HEAD_DIM, HEAD_DIM))), BF16)
    dft_lat = _dft_constants(n)
    dft_ctx = _dft_constants(m)

    for l in range(depth):
        need_ctx = l < depth - 1
        n_tok = t_all if need_ctx else t_lat
        q, k, v, hyp, nq, nk, nv = _proj(
            xf, mods[l], w_in[l].astype(BF16), jnp.tile(q_gain[l], ATT_Q_HEADS)[None, :],
            jnp.tile(k_gain[l], ATT_KV_HEADS)[None, :], gmat, cos_t, sin_t, tile_mod, tile_pos)
        y_att = _gqa(q, k, v, nb, n, m, need_ctx)
        filt = (hy_w1[l], hy_b1[l], hy_freq[l], hy_w2[l], hy_b2[l], hy_w3[l], hy_b3[l])
        gre, gim = _hyena_filter_spectrum(n, dft_lat, *filt)
        cb = hy_conv_b[l][None, :]
        dsk = hy_d[l][None, :]
        y_hy = _hyena(hyp, hy_conv_w[l], cb, dsk, gre, gim, dft_lat, nb, n, 0)
        if need_ctx:
            gre_c, gim_c = _hyena_filter_spectrum(m, dft_ctx, *filt)
            y_hy_c = _hyena(hyp, hy_conv_w[l], cb, dsk, gre_c, gim_c, dft_ctx, nb, m, t_lat // m)
            y_hy = jnp.concatenate([y_hy, y_hy_c], axis=0)
        bias_tab = _na_bias_table(na_rpb[l], grid_w, n // grid_w)
        y_na = _na(nq, nk, nv, bias_tab, nb, n, m, grid_w, need_ctx)

        router_wp = jnp.zeros((d, LANES), F32).at[:, :n_experts].set(router_w[l])
        router_bp = jnp.zeros((1, LANES), F32).at[0, :n_experts].set(router_b[l])
        x1, h2p, idx, wts = _mix_out(xf, y_att, y_hy, y_na, mods[l], w_out[l].astype(BF16), ln1_g[l][None, :],
                                     ln1_b[l][None, :], router_wp, router_bp, tile_mod, n_tok, alpha, n_experts)
        slots, tile_expert, n_used, n_slots = _routing_tables(idx[:, :TOP_K], n_experts, MOE_TILE)
        xs = _dispatch(h2p, slots, n_slots)
        ys = _experts(xs, tile_expert, n_used, exp_w1[l].astype(BF16), exp_b1[l], exp_w2[l].astype(BF16), exp_b2[l])
        xf = _combine(x1, wts, slots, ys, mods[l], ln2_g[l][None, :], ln2_b[l][None, :], tile_mod, alpha)
    return xf[:t_lat].reshape(nb, n, d)


def kernel(x, c, ctx, c_ctx, ada_w, ada_b, w_in, w_out, q_gain, k_gain, hy_conv_w, hy_conv_b, hy_w1, hy_b1,
           hy_freq, hy_w2, hy_b2, hy_w3, hy_b3, hy_d, na_rpb, ln1_g, ln1_b, ln2_g, ln2_b, router_w, router_b,
           exp_w1, exp_b1, exp_w2, exp_b2):
    return _forward(x, c, ctx, c_ctx, ada_w, ada_b, w_in, w_out, q_gain, k_gain, hy_conv_w, hy_conv_b, hy_w1,
                    hy_b1, hy_freq, hy_w2, hy_b2, hy_w3, hy_b3, hy_d, na_rpb, ln1_g, ln1_b, ln2_g, ln2_b,
                    router_w, router_b, exp_w1, exp_b1, exp_w2, exp_b2, grid_w=64)
```

```python
import cmath
import functools
import math

import jax
import jax.numpy as jnp
import numpy as np
from jax import lax
from jax.experimental import pallas as pl
from jax.experimental.pallas import tpu as pltpu

F32 = jnp.float32
BF16 = jnp.bfloat16
HIGHEST = lax.Precision.HIGHEST

HEAD_DIM = 64
ATT_Q_HEADS = 8
ATT_KV_HEADS = 2
ATT_GROUP = ATT_Q_HEADS // ATT_KV_HEADS
ATT_W = ATT_Q_HEADS * HEAD_DIM
KV_W = ATT_KV_HEADS * HEAD_DIM
HY_W = 256
HY_SHORT = 3
HY_EMB = 33
HY_BANDS = (HY_EMB - 1) // 2
HY_FO = 64
HY_MIN_DECAY = math.log(1e-2) / 1.5
HY_MAX_DECAY = math.log(1e-2) / 0.3
NA_HEADS = 4
NA_W = NA_HEADS * HEAD_DIM
NA_WIN_ROWS = 8
NA_WIN_COLS = 16
D_MIX = ATT_W + HY_W + NA_W
D_IN = ATT_W + 2 * KV_W + 3 * HY_W + 3 * NA_W
TOP_K = 4
SWIGLU_ALPHA = 1.702
SWIGLU_LIMIT = 7.0
ROPE_THETA = 10000.0
NORM_EPS = 1e-6
Q_SCALE = HEAD_DIM ** -0.5

LANES = 128
SUBLANES = 8
NEG_BIG = -1e30

TOKEN_TILE = 256
ATT_Q_TILE = 128
NA_Q_ROWS = 8
NA_K_ROWS = 16
FFT_N1 = 8
HY_CB = 128
FREQ_CHUNK = 256
MOE_TILE = 256
VMEM_LIMIT = 56 * 1024 * 1024


def _cparams(sem, vmem=VMEM_LIMIT):
    return pltpu.CompilerParams(dimension_semantics=sem, vmem_limit_bytes=vmem)


def _layer_norm(x):
    mu = jnp.mean(x, axis=-1, keepdims=True)
    xc = x - mu
    var = jnp.mean(xc * xc, axis=-1, keepdims=True)
    return xc * lax.rsqrt(var + NORM_EPS)


def _dot(a, b):
    return jnp.dot(a, b, preferred_element_type=F32)


def _dot_nt(a, b):
    return lax.dot_general(a, b, (((1,), (1,)), ((), ())), preferred_element_type=F32)


def _split_bf16(x):
    hi = x.astype(BF16)
    lo = (x - hi.astype(F32)).astype(BF16)
    return hi, lo


def _dot3(a_hi, a_lo, b_hi, b_lo):
    return _dot(a_hi, b_hi) + _dot(a_hi, b_lo) + _dot(a_lo, b_hi)


def _adaln_kernel(c_ref, w_ref, b_ref, o_ref):
    c = c_ref[...]
    a = c * jax.nn.sigmoid(c)
    o_ref[0] = jnp.dot(a, w_ref[0], preferred_element_type=F32, precision=HIGHEST) + b_ref[0]


def _adaln(cvec, ada_w, ada_b):
    n_layers, d, d6 = ada_w.shape
    bn = d6 // 4 if (d6 // 4) % LANES == 0 else d6
    return pl.pallas_call(
        _adaln_kernel,
        grid=(n_layers, d6 // bn),
        in_specs=[
            pl.BlockSpec((SUBLANES, d), lambda l, j: (0, 0)),
            pl.BlockSpec((1, d, bn), lambda l, j: (l, 0, j)),
            pl.BlockSpec((1, 1, bn), lambda l, j: (l, 0, j)),
        ],
        out_specs=pl.BlockSpec((1, SUBLANES, bn), lambda l, j: (l, 0, j)),
        out_shape=jax.ShapeDtypeStruct((n_layers, SUBLANES, d6), F32),
        compiler_params=_cparams(("arbitrary", "arbitrary")),
        name="adaln",
    )(cvec, ada_w, ada_b.reshape(n_layers, 1, d6))


def _head_norm(a, g_ref, gain):
    w = a.shape[1]
    sq = a * a
    hi, lo = _split_bf16(sq)
    g = g_ref[0:w, 0:w]
    ss = _dot(hi, g) + _dot(lo, g)
    return a * lax.rsqrt(ss * (1.0 / HEAD_DIM) + NORM_EPS) * gain


def _rope(a, cos, sin_signed):
    w = a.shape[1]
    reps = w // LANES
    c = jnp.concatenate([cos] * reps, axis=1) if reps > 1 else cos
    s = jnp.concatenate([sin_signed] * reps, axis=1) if reps > 1 else sin_signed
    lane = lax.broadcasted_iota(jnp.int32, a.shape, 1)
    first = (lane & 31) < 16
    rot = jnp.where(first, pltpu.roll(a, w - 16, axis=1), pltpu.roll(a, 16, axis=1))
    return a * c + rot * s


def _proj_kernel(tmod_ref, tpos_ref, x_ref, mod_ref, w_ref, qg_ref, kg_ref, g_ref, cos_ref, sin_ref,
                 q_out, k_out, v_out, hy_out, nq_out, nk_out, nv_out):
    del tmod_ref, tpos_ref
    h = _layer_norm(x_ref[...]) * (1.0 + mod_ref[0, 1:2, :]) + mod_ref[0, 0:1, :]
    p = _dot(h.astype(BF16), w_ref[...])
    o = 0
    aq = p[:, o:o + ATT_W]; o += ATT_W
    ak = p[:, o:o + KV_W]; o += KV_W
    av = p[:, o:o + KV_W]; o += KV_W
    hy = p[:, o:o + 3 * HY_W]; o += 3 * HY_W
    nq = p[:, o:o + NA_W]; o += NA_W
    nk = p[:, o:o + NA_W]; o += NA_W
    nv = p[:, o:o + NA_W]
    cos = cos_ref[...]
    sin = sin_ref[...]
    q = _rope(_head_norm(aq, g_ref, qg_ref[...]), cos, sin) * Q_SCALE
    k = _rope(_head_norm(ak, g_ref, kg_ref[...]), cos, sin)
    q_out[...] = q.astype(BF16)
    k_out[...] = k.astype(BF16)
    v_out[...] = av.astype(BF16)
    hy_out[...] = hy
    nq_out[...] = (nq * Q_SCALE).astype(BF16)
    nk_out[...] = nk.astype(BF16)
    nv_out[...] = nv.astype(BF16)


def _proj(xf, mods_l, w_in_bf, q_gain, k_gain, gmat, cos_t, sin_t, tile_mod, tile_pos):
    t, d = xf.shape
    tm = TOKEN_TILE
    n_tiles = t // tm
    row = lambda i, tmod, tpos: (i, 0)
    const = lambda i, tmod, tpos: (0, 0)
    widths = (ATT_W, KV_W, KV_W, 3 * HY_W, NA_W, NA_W, NA_W)
    dtypes = (BF16, BF16, BF16, F32, BF16, BF16, BF16)
    grid_spec = pltpu.PrefetchScalarGridSpec(
        num_scalar_prefetch=2,
        grid=(n_tiles,),
        in_specs=[
            pl.BlockSpec((tm, d), row),
            pl.BlockSpec((1, 6, d), lambda i, tmod, tpos: (tmod[i], 0, 0)),
            pl.BlockSpec((d, D_IN), const),
            pl.BlockSpec((1, ATT_W), const),
            pl.BlockSpec((1, KV_W), const),
            pl.BlockSpec((ATT_W, ATT_W), const),
            pl.BlockSpec((tm, LANES), lambda i, tmod, tpos: (tpos[i], 0)),
            pl.BlockSpec((tm, LANES), lambda i, tmod, tpos: (tpos[i], 0)),
        ],
        out_specs=[pl.BlockSpec((tm, w), row) for w in widths],
    )
    return pl.pallas_call(
        _proj_kernel,
        grid_spec=grid_spec,
        out_shape=[jax.ShapeDtypeStruct((t, w), dt) for w, dt in zip(widths, dtypes)],
        compiler_params=_cparams(("arbitrary",)),
        name="ln_mod_proj",
    )(tile_mod, tile_pos, xf, mods_l, w_in_bf, q_gain, k_gain, gmat, cos_t, sin_t)


def _attend(q, segs, n_kv, grp, bias=None):
    tq = q.shape[0]
    hd = HEAD_DIM
    outs = []
    for j in range(n_kv):
        heads = [q[:, (j * grp + g) * hd:(j * grp + g + 1) * hd] for g in range(grp)]
        qs = jnp.concatenate(heads, axis=0) if grp > 1 else heads[0]
        scores = []
        for si, (k, _) in enumerate(segs):
            s = _dot_nt(qs, k[:, j * hd:(j + 1) * hd])
            if bias is not None and bias[j][si] is not None:
                s = s + bias[j][si]
            scores.append(s)
        m = scores[0].max(axis=-1, keepdims=True)
        for s in scores[1:]:
            m = jnp.maximum(m, s.max(axis=-1, keepdims=True))
        denom = None
        acc = None
        for s, (_, v) in zip(scores, segs):
            p = jnp.exp(s - m)
            ps = p.sum(axis=-1, keepdims=True)
            pv = _dot(p.astype(BF16), v[:, j * hd:(j + 1) * hd])
            denom = ps if denom is None else denom + ps
            acc = pv if acc is None else acc + pv
        o = acc / denom
        for g in range(grp):
            outs.append(o[g * tq:(g + 1) * tq])
    return jnp.concatenate(outs, axis=1)


def _gqa_kernel(q_ref, kl_ref, vl_ref, kc_ref, vc_ref, o_ref):
    segs = [(kl_ref[...], vl_ref[...]), (kc_ref[...], vc_ref[...])]
    o_ref[...] = _attend(q_ref[...], segs, ATT_KV_HEADS, ATT_GROUP).astype(o_ref.dtype)


def _gqa(q, k, v, nb, n, m, with_ctx):
    t = q.shape[0]
    tq = ATT_Q_TILE
    n_lat = n // tq
    ctx_blk0 = nb * n // m
    qmap = lambda b, i: (b * n_lat + i, 0)
    lat = lambda b, i: (b, 0)
    ctx = lambda b, i: (ctx_blk0 + b, 0)
    y = pl.pallas_call(
        _gqa_kernel,
        grid=(nb, n_lat),
        in_specs=[
            pl.BlockSpec((tq, ATT_W), qmap),
            pl.BlockSpec((n, KV_W), lat),
            pl.BlockSpec((n, KV_W), lat),
            pl.BlockSpec((m, KV_W), ctx),
            pl.BlockSpec((m, KV_W), ctx),
        ],
        out_specs=pl.BlockSpec((tq, ATT_W), qmap),
        out_shape=jax.ShapeDtypeStruct((nb * n, ATT_W), BF16),
        compiler_params=_cparams(("arbitrary", "arbitrary")),
        name="gqa_attention",
    )(q, k, v, k, v)
    if not with_ctx:
        return y
    yc = _ctx_attention(q, k, v, nb, n, m, ATT_KV_HEADS, ATT_GROUP, ATT_W, "gqa_ctx_attention")
    return jnp.concatenate([y, yc], axis=0)


def _na_kernel(q_ref, kl_ref, vl_ref, kc_ref, vc_ref, bias_ref, o_ref, *, grid_w, grid_rows):
    i = pl.program_id(1)
    krow0 = jnp.clip(i * NA_Q_ROWS - NA_WIN_ROWS // 2, 0, grid_rows - NA_K_ROWS)
    start = pl.multiple_of(krow0 * grid_w, 4 * grid_w)
    nk = NA_K_ROWS * grid_w
    segs = [(kl_ref[pl.ds(start, nk), :], vl_ref[pl.ds(start, nk), :]), (kc_ref[...], vc_ref[...])]
    bias = [[bias_ref[0, h], None] for h in range(NA_HEADS)]
    o_ref[...] = _attend(q_ref[...], segs, NA_HEADS, 1, bias).astype(o_ref.dtype)


def _na_bias_table(rpb, grid_w, grid_rows):
    qr, kr = NA_Q_ROWS, NA_K_ROWS
    n_rr, n_rc = 2 * NA_WIN_ROWS - 1, 2 * NA_WIN_COLS - 1
    c = np.arange(grid_w)[:, None]
    kcol = np.arange(grid_w)[None, :]
    cs = np.clip(c - NA_WIN_COLS // 2, 0, grid_w - NA_WIN_COLS)
    col_ok = (kcol >= cs) & (kcol < cs + NA_WIN_COLS)
    csel = ((kcol - c + NA_WIN_COLS - 1)[:, :, None] == np.arange(n_rc)) & col_ok[:, :, None]
    rsel, mask = [], []
    for r0 in (0, qr, grid_rows - qr):
        k0 = int(np.clip(r0 - NA_WIN_ROWS // 2, 0, grid_rows - kr))
        r = r0 + np.arange(qr)[:, None]
        krow = k0 + np.arange(kr)[None, :]
        rs = np.clip(r - NA_WIN_ROWS // 2, 0, grid_rows - NA_WIN_ROWS)
        row_ok = (krow >= rs) & (krow < rs + NA_WIN_ROWS)
        rsel.append(((krow - r + NA_WIN_ROWS - 1)[:, :, None] == np.arange(n_rr)) & row_ok[:, :, None])
        ok = row_ok[:, None, :, None] & col_ok[None, :, None, :]
        mask.append(np.where(ok, 0.0, NEG_BIG))
    rsel = jnp.asarray(np.stack(rsel), F32)
    mask = jnp.asarray(np.stack(mask), F32)
    vals = jnp.einsum("tjia,hab,ckb->thjcik", rsel, rpb, jnp.asarray(csel, F32), precision=HIGHEST)
    return (vals + mask[:, None]).reshape(3, NA_HEADS, qr * grid_w, kr * grid_w)


def _na(nq, nk, nv, bias_tab, nb, n, m, grid_w, with_ctx):
    t = nq.shape[0]
    grid_rows = n // grid_w
    tq = NA_Q_ROWS * grid_w
    n_lat = n // tq
    ctx_blk0 = nb * n // m
    lat = lambda b, i: (b, 0)
    ctx = lambda b, i: (ctx_blk0 + b, 0)
    qmap = lambda b, i: (b * n_lat + i, 0)

    def bmap(b, i):
        return (jnp.where(i == 0, 0, jnp.where(i >= n_lat - 1, 2, 1)), 0, 0, 0)

    y = pl.pallas_call(
        functools.partial(_na_kernel, grid_w=grid_w, grid_rows=grid_rows),
        grid=(nb, n_lat),
        in_specs=[
            pl.BlockSpec((tq, NA_W), qmap),
            pl.BlockSpec((n, NA_W), lat),
            pl.BlockSpec((n, NA_W), lat),
            pl.BlockSpec((m, NA_W), ctx),
            pl.BlockSpec((m, NA_W), ctx),
            pl.BlockSpec((1, NA_HEADS, tq, NA_K_ROWS * grid_w), bmap),
        ],
        out_specs=pl.BlockSpec((tq, NA_W), qmap),
        out_shape=jax.ShapeDtypeStruct((nb * n, NA_W), BF16),
        compiler_params=_cparams(("arbitrary", "arbitrary")),
        name="neighbourhood_attention",
    )(nq, nk, nv, nk, nv, bias_tab)
    if not with_ctx:
        return y
    yc = _ctx_attention(nq, nk, nv, nb, n, m, NA_HEADS, 1, NA_W, "na_ctx_attention")
    return jnp.concatenate([y, yc], axis=0)


def _ctx_attn_kernel(q_ref, k_ref, v_ref, o_ref, *, n_kv, grp):
    segs = [(k_ref[...], v_ref[...])]
    o_ref[...] = _attend(q_ref[...], segs, n_kv, grp).astype(o_ref.dtype)


def _ctx_attention(q, k, v, nb, n, m, n_kv, grp, width, name):
    ctx_blk0 = nb * n // m
    ctx = lambda b: (ctx_blk0 + b, 0)
    kw = n_kv * HEAD_DIM
    return pl.pallas_call(
        functools.partial(_ctx_attn_kernel, n_kv=n_kv, grp=grp),
        grid=(nb,),
        in_specs=[
            pl.BlockSpec((m, width), ctx),
            pl.BlockSpec((m, kw), ctx),
            pl.BlockSpec((m, kw), ctx),
        ],
        out_specs=pl.BlockSpec((m, width), lambda b: (b, 0)),
        out_shape=jax.ShapeDtypeStruct((nb * m, width), BF16),
        compiler_params=_cparams(("arbitrary",)),
        name=name,
    )(q, k, v)


def _dft_constants(seq_len):
    n = 2 * seq_len
    n2 = n // FFT_N1
    idx = np.arange(n2, dtype=np.int64)
    ang = 2.0 * np.pi * ((idx[:, None] * idx[None, :]) % n2).astype(np.float64) / n2
    cmat, smat = np.cos(ang), np.sin(ang)

    def split(a):
        hi = jnp.asarray(a, F32).astype(BF16)
        lo = (jnp.asarray(a, F32) - hi.astype(F32)).astype(BF16)
        return hi, lo

    def col_chunks(a):
        return np.transpose(a.reshape(a.shape[0], n2 // chunk, chunk), (1, 0, 2))

    chunk = min(FREQ_CHUNK, n2)
    tw_ang = 2.0 * np.pi * (idx[:, None] * np.arange(FFT_N1)[None, :]).astype(np.float64) / n
    return dict(
        n2=n2, chunk=chunk,
        fwd_c=split(cmat), fwd_s=split(smat),
        inv_c=split(col_chunks(cmat[: n2 // 2] / n)),
        inv_s=split(col_chunks(smat[: n2 // 2] / n)),
        tw_re=jnp.asarray(np.cos(tw_ang), F32), tw_im=jnp.asarray(-np.sin(tw_ang), F32),
    )


def _cmul_const(a, w):
    re, im = a
    if abs(w - 1) < 1e-12:
        return a
    if abs(w + 1) < 1e-12:
        return (-re, -im)
    if abs(w + 1j) < 1e-12:
        return (im, -re)
    if abs(w - 1j) < 1e-12:
        return (-im, re)
    return (re * w.real - im * w.imag, re * w.imag + im * w.real)


def _fft_blocks(xs, sign):
    n = len(xs)
    if n == 1:
        return xs
    even = _fft_blocks(xs[0::2], sign)
    odd = _fft_blocks(xs[1::2], sign)
    out = [None] * n
    for k in range(n // 2):
        t = _cmul_const(odd[k], cmath.exp(sign * 2j * cmath.pi * k / n))
        out[k] = (even[k][0] + t[0], even[k][1] + t[1])
        out[k + n // 2] = (even[k][0] - t[0], even[k][1] - t[1])
    return out


def _lane_blocks(a):
    return [a[:, j * LANES:(j + 1) * LANES] for j in range(a.shape[1] // LANES)]


def _forward_spectrum(xh, xl, rows, ch_ref, cl_ref, sh_ref, sl_ref, twr_ref, twi_ref):
    a_re = _dot3(ch_ref[rows, :], cl_ref[rows, :], xh, xl)
    a_im = -_dot3(sh_ref[rows, :], sl_ref[rows, :], xh, xl)
    blocks = []
    for s1, (re, im) in enumerate(zip(_lane_blocks(a_re), _lane_blocks(a_im))):
        tr = twr_ref[rows, s1:s1 + 1]
        ti = twi_ref[rows, s1:s1 + 1]
        blocks.append((re * tr - im * ti, re * ti + im * tr))
    return _fft_blocks(blocks, -1)


def _filter_kernel(emb_ref, w1_ref, b1_ref, fr_ref, w2_ref, b2_ref, w3f_ref, w3b_ref, b3f_ref, b3b_ref,
                   dec_ref, ch_ref, cl_ref, sh_ref, sl_ref, twr_ref, twi_ref, gre_ref, gim_ref, g_scr, *,
                   seq_len, chunk):
    n = 2 * seq_len
    n2 = n // FFT_N1
    emb = emb_ref[...]
    hid = jnp.sin(fr_ref[0:1, :] * (jnp.dot(emb, w1_ref[...], preferred_element_type=F32, precision=HIGHEST)
                                    + b1_ref[...]))
    hid = jnp.sin(fr_ref[1:2, :] * (jnp.dot(hid, w2_ref[...], preferred_element_type=F32, precision=HIGHEST)
                                    + b2_ref[...]))
    f_fwd = jnp.dot(hid, w3f_ref[...], preferred_element_type=F32, precision=HIGHEST) + b3f_ref[...]
    f_bwd = jnp.dot(hid, w3b_ref[...], preferred_element_type=F32, precision=HIGHEST) + b3b_ref[...]
    row = lax.broadcasted_iota(jnp.int32, f_fwd.shape, 0)
    t_pos = emb[:, 0:1]
    g = jnp.where(row < seq_len, f_fwd, jnp.where(row > seq_len, f_bwd, 0.0))
    g = g * jnp.exp(-t_pos * jnp.abs(dec_ref[...]))
    g = g * lax.rsqrt(jnp.sum(g * g, axis=0, keepdims=True) + NORM_EPS)
    g_scr[...] = g
    x = jnp.concatenate([g_scr[pl.ds(s1, n2, stride=FFT_N1), :] for s1 in range(FFT_N1)], axis=1)
    xh, xl = _split_bf16(x)

    def freq_chunk(ci, carry):
        rows = pl.ds(pl.multiple_of(ci * chunk, chunk), chunk)
        spec = _forward_spectrum(xh, xl, rows, ch_ref, cl_ref, sh_ref, sl_ref, twr_ref, twi_ref)
        gre_ref[0, rows, :] = jnp.concatenate([b[0] for b in spec], axis=1)
        gim_ref[0, rows, :] = jnp.concatenate([b[1] for b in spec], axis=1)
        return carry

    lax.fori_loop(0, n2 // chunk, freq_chunk, 0)


def _filter_embedding(seq_len):
    n = 2 * seq_len
    pos = np.concatenate([np.arange(seq_len), [0], np.arange(seq_len - 1, 0, -1)])
    t = np.linspace(0.0, 1.0, seq_len, dtype=np.float32)[pos][:, None]
    w = ((2.0 * math.pi / seq_len) * np.arange(seq_len, dtype=np.float32))[pos][:, None]
    bands = np.linspace(1e-4, HY_BANDS - 1, HY_BANDS, dtype=np.float32)[None, :]
    z = np.concatenate([t, np.cos(bands * w), -np.sin(bands * w)], axis=-1).astype(np.float32)
    emb = np.zeros((n, LANES), np.float32)
    emb[:, :HY_EMB] = z
    return jnp.asarray(emb)


def _hyena_filter_spectrum(seq_len, consts, f_w1, f_b1, f_freq, f_w2, f_b2, f_w3, f_b3):
    n = 2 * seq_len
    n2 = consts["n2"]
    ncb = HY_W // HY_CB
    emb = _filter_embedding(seq_len)
    w1p = jnp.zeros((LANES, HY_FO), F32).at[:HY_EMB].set(f_w1)
    deltas = jnp.asarray(np.linspace(HY_MIN_DECAY, HY_MAX_DECAY, HY_W, dtype=np.float32))[None, :]
    const = lambda j: (0, 0)
    out_sds = jax.ShapeDtypeStruct((ncb, n2, FFT_N1 * HY_CB), F32)
    return pl.pallas_call(
        functools.partial(_filter_kernel, seq_len=seq_len, chunk=consts["chunk"]),
        grid=(ncb,),
        in_specs=[
            pl.BlockSpec((n, LANES), const),
            pl.BlockSpec((LANES, HY_FO), const),
            pl.BlockSpec((1, HY_FO), const),
            pl.BlockSpec((2, HY_FO), const),
            pl.BlockSpec((HY_FO, HY_FO), const),
            pl.BlockSpec((1, HY_FO), const),
            pl.BlockSpec((HY_FO, HY_CB), lambda j: (0, j)),
            pl.BlockSpec((HY_FO, HY_CB), lambda j: (0, ncb + j)),
            pl.BlockSpec((1, HY_CB), lambda j: (0, j)),
            pl.BlockSpec((1, HY_CB), lambda j: (0, ncb + j)),
            pl.BlockSpec((1, HY_CB), lambda j: (0, j)),
        ] + [pl.BlockSpec((n2, n2), const, pipeline_mode=pl.Buffered(1))] * 4
          + [pl.BlockSpec((n2, FFT_N1), const, pipeline_mode=pl.Buffered(1))] * 2,
        out_specs=[pl.BlockSpec((1, n2, FFT_N1 * HY_CB), lambda j: (j, 0, 0))] * 2,
        out_shape=[out_sds, out_sds],
        scratch_shapes=[pltpu.VMEM((n, HY_CB), F32)],
        compiler_params=_cparams(("arbitrary",)),
        name=f"hyena_filter_{seq_len}",
    )(emb, w1p, f_b1[None, :], f_freq, f_w2, f_b2[None, :], f_w3, f_w3, f_b3[None, :], f_b3[None, :], deltas,
      *consts["fwd_c"], *consts["fwd_s"], consts["tw_re"], consts["tw_im"])


def _short_conv(p, w_ref, b_ref):
    n = p.shape[0]
    row = lax.broadcasted_iota(jnp.int32, p.shape, 0)
    prev = jnp.where(row == 0, 0.0, pltpu.roll(p, 1, axis=0))
    nxt = jnp.where(row == n - 1, 0.0, pltpu.roll(p, n - 1, axis=0))
    return prev * w_ref[0:1, :] + p * w_ref[1:2, :] + nxt * w_ref[2:3, :] + b_ref[...]


def _hyena_kernel(p0_ref, p1_ref, p2_ref, w0_ref, w1_ref, w2_ref, b0_ref, b1_ref, b2_ref, d_ref,
                  gre_ref, gim_ref, fch_ref, fcl_ref, fsh_ref, fsl_ref, ich_ref, icl_ref, ish_ref, isl_ref,
                  twr_ref, twi_ref, o_ref, z_scr, y_scr, acc_scr, *, seq_len, chunk):
    n2 = 2 * seq_len // FFT_N1
    half = n2 // 2
    x1 = _short_conv(p1_ref[...], w1_ref, b1_ref)
    v = _short_conv(p2_ref[...], w2_ref, b2_ref)
    z_scr[...] = x1 * v
    x = jnp.concatenate([z_scr[pl.ds(s1, half, stride=FFT_N1), :] for s1 in range(FFT_N1)], axis=1)
    xh, xl = _split_bf16(x)
    acc_scr[...] = jnp.zeros(acc_scr.shape, F32)

    def freq_chunk(ci, carry):
        rows = pl.ds(pl.multiple_of(ci * chunk, chunk), chunk)
        zf = _forward_spectrum(xh, xl, rows, fch_ref, fcl_ref, fsh_ref, fsl_ref, twr_ref, twi_ref)
        gre = _lane_blocks(gre_ref[0, rows, :])
        gim = _lane_blocks(gim_ref[0, rows, :])
        prod = [(zr * gr - zi * gi, zr * gi + zi * gr) for (zr, zi), gr, gi in zip(zf, gre, gim)]
        back = _fft_blocks(prod, +1)
        c_re, c_im = [], []
        for s1, (re, im) in enumerate(back):
            tr = twr_ref[rows, s1:s1 + 1]
            ti = -twi_ref[rows, s1:s1 + 1]
            c_re.append(re * tr - im * ti)
            c_im.append(re * ti + im * tr)
        rh, rl = _split_bf16(jnp.concatenate(c_re, axis=1))
        ih, il = _split_bf16(jnp.concatenate(c_im, axis=1))
        acc_scr[...] += _dot3(ich_ref[ci], icl_ref[ci], rh, rl) - _dot3(ish_ref[ci], isl_ref[ci], ih, il)
        return carry

    lax.fori_loop(0, n2 // chunk, freq_chunk, 0)
    for s1 in range(FFT_N1):
        y_scr[pl.ds(s1, half, stride=FFT_N1), :] = acc_scr[:, s1 * LANES:(s1 + 1) * LANES]
    x0 = _short_conv(p0_ref[...], w0_ref, b0_ref)
    o_ref[...] = (x0 * (y_scr[...] + z_scr[...] * d_ref[...])).astype(o_ref.dtype)


def _hyena(hyp, conv_w, conv_b, d_skip, gre, gim, consts, nb, seq_len, row_blk0):
    n2 = consts["n2"]
    chunk = consts["chunk"]
    ncb = HY_W // HY_CB
    half = n2 // 2
    const = lambda j, b: (0, 0)
    once = pl.Buffered(1)
    in_specs = (
        [pl.BlockSpec((seq_len, HY_CB), (lambda j, b, g=g: (row_blk0 + b, g * ncb + j))) for g in range(3)]
        + [pl.BlockSpec((HY_SHORT, HY_CB), (lambda j, b, g=g: (0, g * ncb + j))) for g in range(3)]
        + [pl.BlockSpec((1, HY_CB), (lambda j, b, g=g: (0, g * ncb + j))) for g in range(3)]
        + [pl.BlockSpec((1, HY_CB), lambda j, b: (0, j))]
        + [pl.BlockSpec((1, n2, FFT_N1 * HY_CB), lambda j, b: (j, 0, 0), pipeline_mode=once)] * 2
        + [pl.BlockSpec((n2, half), const, pipeline_mode=once)] * 4
        + [pl.BlockSpec((n2 // chunk, half, chunk), lambda j, b: (0, 0, 0), pipeline_mode=once)] * 4
        + [pl.BlockSpec((n2, FFT_N1), const, pipeline_mode=once)] * 2
    )
    fwd = [a[:, :half] for a in (*consts["fwd_c"], *consts["fwd_s"])]
    args = [hyp, hyp, hyp, conv_w, conv_w, conv_w, conv_b, conv_b, conv_b, d_skip, gre, gim,
            *fwd, *consts["inv_c"], *consts["inv_s"], consts["tw_re"], consts["tw_im"]]
    return pl.pallas_call(
        functools.partial(_hyena_kernel, seq_len=seq_len, chunk=chunk),
        grid=(ncb, nb),
        in_specs=in_specs,
        out_specs=pl.BlockSpec((seq_len, HY_CB), lambda j, b: (b, j)),
        out_shape=jax.ShapeDtypeStruct((nb * seq_len, HY_W), BF16),
        scratch_shapes=[pltpu.VMEM((seq_len, HY_CB), F32), pltpu.VMEM((seq_len, HY_CB), F32),
                        pltpu.VMEM((half, FFT_N1 * HY_CB), F32)],
        compiler_params=_cparams(("arbitrary", "arbitrary")),
        name=f"hyena_mixer_{seq_len}",
    )(*args)


def _pack_pairs(h):
    half = h.shape[1] // 2
    lo = pltpu.bitcast(h[:, :half].astype(BF16).astype(F32), jnp.uint32) >> 16
    hi = pltpu.bitcast(h[:, half:].astype(BF16).astype(F32), jnp.uint32) & jnp.uint32(0xFFFF0000)
    return lo | hi


def _unpack_pairs(u):
    lo = pltpu.bitcast(u << 16, F32)
    hi = pltpu.bitcast(u & jnp.uint32(0xFFFF0000), F32)
    return jnp.concatenate([lo, hi], axis=1).astype(BF16)


def _mix_out_kernel(tmod_ref, x_ref, ya_ref, yh_ref, yn_ref, mod_ref, w_ref, g_ref, b_ref, rwh_ref, rwl_ref, rb_ref,
                    x1_out, h2_out, idx_out, wt_out, *, alpha, n_experts):
    del tmod_ref
    y = _dot(jnp.concatenate([ya_ref[...], yh_ref[...], yn_ref[...]], axis=1), w_ref[...])
    x1 = _layer_norm(alpha * x_ref[...] + mod_ref[0, 2:3, :] * y) * g_ref[...] + b_ref[...]
    x1_out[...] = x1
    h2 = _layer_norm(x1) * (1.0 + mod_ref[0, 4:5, :]) + mod_ref[0, 3:4, :]
    h2_out[...] = _pack_pairs(h2)
    h2_hi, h2_lo = _split_bf16(h2)
    logits = _dot3(h2_hi, h2_lo, rwh_ref[...], rwl_ref[...]) + rb_ref[...]
    lane = lax.broadcasted_iota(jnp.int32, logits.shape, 1)
    lane_f = lane.astype(F32)
    cur = jnp.where(lane < n_experts, logits, -jnp.inf)
    vals, idxs = [], []
    for _ in range(TOP_K):
        m = cur.max(axis=-1, keepdims=True)
        ix = jnp.where(cur == m, lane_f, float(LANES)).min(axis=-1, keepdims=True)
        vals.append(m)
        idxs.append(ix)
        cur = jnp.where(lane_f == ix, -jnp.inf, cur)
    es = [jnp.exp(v - vals[0]) for v in vals]
    denom = es[0]
    for e in es[1:]:
        denom = denom + e
    idx_v = jnp.zeros(logits.shape, F32)
    wt_v = jnp.zeros(logits.shape, F32)
    for k in range(TOP_K):
        idx_v = jnp.where(lane == k, idxs[k], idx_v)
        wt_v = jnp.where(lane == k, es[k] / denom, wt_v)
    idx_out[...] = idx_v.astype(jnp.int32)
    wt_out[...] = wt_v


def _mix_out(xf, y_att, y_hy, y_na, mods_l, w_out_bf, ln_g, ln_b, router_wp, router_bp, tile_mod, n_tok,
             alpha, n_experts):
    d = xf.shape[1]
    tm = TOKEN_TILE
    row = lambda i, tmod: (i, 0)
    const = lambda i, tmod: (0, 0)
    grid_spec = pltpu.PrefetchScalarGridSpec(
        num_scalar_prefetch=1,
        grid=(n_tok // tm,),
        in_specs=[
            pl.BlockSpec((tm, d), row),
            pl.BlockSpec((tm, ATT_W), row),
            pl.BlockSpec((tm, HY_W), row),
            pl.BlockSpec((tm, NA_W), row),
            pl.BlockSpec((1, 6, d), lambda i, tmod: (tmod[i], 0, 0)),
            pl.BlockSpec((D_MIX, d), const),
            pl.BlockSpec((1, d), const),
            pl.BlockSpec((1, d), const),
            pl.BlockSpec((d, LANES), const),
            pl.BlockSpec((d, LANES), const),
            pl.BlockSpec((1, LANES), const),
        ],
        out_specs=[
            pl.BlockSpec((tm, d), row),
            pl.BlockSpec((tm, d // 2), row),
            pl.BlockSpec((tm, LANES), row),
            pl.BlockSpec((tm, LANES), row),
        ],
    )
    return pl.pallas_call(
        functools.partial(_mix_out_kernel, alpha=alpha, n_experts=n_experts),
        grid_spec=grid_spec,
        out_shape=[
            jax.ShapeDtypeStruct((n_tok, d), F32),
            jax.ShapeDtypeStruct((n_tok, d // 2), jnp.uint32),
            jax.ShapeDtypeStruct((n_tok, LANES), jnp.int32),
            jax.ShapeDtypeStruct((n_tok, LANES), F32),
        ],
        compiler_params=_cparams(("arbitrary",)),
        name="mix_out_norm_route",
    )(tile_mod, xf, y_att, y_hy, y_na, mods_l, w_out_bf, ln_g, ln_b, *_split_bf16(router_wp), router_bp)


def _dispatch_kernel(slot_ref, h_ref, xs_in, xs_hbm, sem, *, tm):
    del xs_in

    def row_copy(src_row, dst_row):
        return pltpu.make_async_copy(h_ref.at[pl.ds(src_row, 1)], xs_hbm.at[pl.ds(dst_row, 1)], sem)

    def issue(t, carry):
        for k in range(TOP_K):
            row_copy(t, slot_ref[t * TOP_K + k]).start(priority=k % 2)
        return carry

    lax.fori_loop(0, tm, issue, 0, unroll=8)

    def drain(t, carry):
        for k in range(TOP_K):
            row_copy(0, 0).wait()
        return carry

    lax.fori_loop(0, tm, drain, 0, unroll=8)


def _dispatch(h2p, slots_flat, n_slots):
    n_tok, hw = h2p.shape
    tm = TOKEN_TILE
    xs0 = jnp.zeros((n_slots, hw), h2p.dtype)
    return pl.pallas_call(
        functools.partial(_dispatch_kernel, tm=tm),
        grid=(n_tok // tm,),
        in_specs=[
            pl.BlockSpec((tm * TOP_K,), lambda i: (i,), memory_space=pltpu.SMEM),
            pl.BlockSpec((tm, hw), lambda i: (i, 0)),
            pl.BlockSpec(memory_space=pl.ANY),
        ],
        out_specs=pl.BlockSpec(memory_space=pl.ANY),
        out_shape=jax.ShapeDtypeStruct((n_slots, hw), h2p.dtype),
        scratch_shapes=[pltpu.SemaphoreType.DMA],
        input_output_aliases={2: 0},
        compiler_params=_cparams(("arbitrary",)),
        name="moe_dispatch",
    )(slots_flat, h2p, xs0)


def _expert_kernel(te_ref, nu_ref, x_ref, w1_ref, b1_ref, w2_ref, b2_ref, o_ref, w1_bf, w2_bf, *, d_expert):
    i = pl.program_id(0)

    @pl.when((i == 0) | (te_ref[i] != te_ref[jnp.maximum(i - 1, 0)]))
    def _():
        w1_bf[...] = w1_ref[0].astype(BF16)
        w2_bf[...] = w2_ref[0].astype(BF16)

    @pl.when(i < nu_ref[0])
    def _():
        x = _unpack_pairs(x_ref[...])
        hid = _dot(x, w1_bf[...]) + b1_ref[0]
        glu = jnp.minimum(hid[:, :d_expert], SWIGLU_LIMIT)
        lin = jnp.clip(hid[:, d_expert:], -SWIGLU_LIMIT, SWIGLU_LIMIT)
        act = glu * jax.nn.sigmoid(SWIGLU_ALPHA * glu) * (lin + 1.0)
        o_ref[...] = _dot(act.astype(BF16), w2_bf[...]) + b2_ref[0]

    @pl.when(i >= nu_ref[0])
    def _():
        o_ref[...] = jnp.zeros(o_ref.shape, o_ref.dtype)


def _experts(xs, tile_expert, n_used, w1, b1, w2, b2, layer):
    n_slots, hw = xs.shape
    n_exp = b1.shape[0]
    _, d, d2 = w1.shape
    d_expert = d2 // 2
    tm = MOE_TILE
    n_tiles = n_slots // tm
    e0 = layer * n_exp
    grid_spec = pltpu.PrefetchScalarGridSpec(
        num_scalar_prefetch=2,
        grid=(n_tiles,),
        in_specs=[
            pl.BlockSpec((tm, hw), lambda i, te, nu: (i, 0)),
            pl.BlockSpec((1, d, d2), lambda i, te, nu: (e0 + te[i], 0, 0)),
            pl.BlockSpec((1, 1, d2), lambda i, te, nu: (te[i], 0, 0)),
            pl.BlockSpec((1, d_expert, d), lambda i, te, nu: (e0 + te[i], 0, 0)),
            pl.BlockSpec((1, 1, d), lambda i, te, nu: (te[i], 0, 0)),
        ],
        out_specs=pl.BlockSpec((tm, d), lambda i, te, nu: (i, 0)),
        scratch_shapes=[pltpu.VMEM((d, d2), BF16), pltpu.VMEM((d_expert, d), BF16)],
    )
    return pl.pallas_call(
        functools.partial(_expert_kernel, d_expert=d_expert),
        grid_spec=grid_spec,
        out_shape=jax.ShapeDtypeStruct((n_slots, d), F32),
        compiler_params=_cparams(("arbitrary",)),
        name="moe_experts",
    )(tile_expert, n_used, xs, w1, b1.reshape(n_exp, 1, d2), w2, b2.reshape(n_exp, 1, d))


def _combine_kernel(tmod_ref, slot_ref, x1_ref, wt_ref, mod_ref, g_ref, b_ref, ys_hbm, o_ref, buf, sem, *, tm, alpha):
    del tmod_ref

    def row_copy(src_row, k, t):
        return pltpu.make_async_copy(ys_hbm.at[pl.ds(src_row, 1)], buf.at[k, pl.ds(t, 1)], sem)

    def issue(t, carry):
        for k in range(TOP_K):
            row_copy(slot_ref[t * TOP_K + k], k, t).start(priority=k % 2)
        return carry

    lax.fori_loop(0, tm, issue, 0, unroll=8)

    def drain(t, carry):
        for k in range(TOP_K):
            row_copy(0, k, t).wait()
        return carry

    lax.fori_loop(0, tm, drain, 0, unroll=8)
    wt = wt_ref[...]
    moe = buf[0] * wt[:, 0:1]
    for k in range(1, TOP_K):
        moe = moe + buf[k] * wt[:, k:k + 1]
    o_ref[...] = _layer_norm(alpha * x1_ref[...] + mod_ref[0, 5:6, :] * moe) * g_ref[...] + b_ref[...]


def _combine(x1, wts, slots_flat, ys, mods_l, ln_g, ln_b, tile_mod, alpha):
    n_tok, d = x1.shape
    tm = TOKEN_TILE
    row = lambda i, tmod: (i, 0)
    const = lambda i, tmod: (0, 0)
    grid_spec = pltpu.PrefetchScalarGridSpec(
        num_scalar_prefetch=1,
        grid=(n_tok // tm,),
        in_specs=[
            pl.BlockSpec((tm * TOP_K,), lambda i, tmod: (i,), memory_space=pltpu.SMEM),
            pl.BlockSpec((tm, d), row),
            pl.BlockSpec((tm, LANES), row),
            pl.BlockSpec((1, 6, d), lambda i, tmod: (tmod[i], 0, 0)),
            pl.BlockSpec((1, d), const),
            pl.BlockSpec((1, d), const),
            pl.BlockSpec(memory_space=pl.ANY),
        ],
        out_specs=pl.BlockSpec((tm, d), row),
        scratch_shapes=[pltpu.VMEM((TOP_K, tm, d), F32), pltpu.SemaphoreType.DMA],
    )
    return pl.pallas_call(
        functools.partial(_combine_kernel, tm=tm, alpha=alpha),
        grid_spec=grid_spec,
        out_shape=jax.ShapeDtypeStruct((n_tok, d), F32),
        compiler_params=_cparams(("arbitrary",)),
        name="moe_combine_norm",
    )(tile_mod, slots_flat, x1, wts, mods_l, ln_g, ln_b, ys)


def _routing_tables(idx4, n_experts, tile):
    n_tok = idx4.shape[0]
    blk = TOKEN_TILE
    hit = idx4[:, :, None] == jnp.arange(n_experts, dtype=jnp.int32)[None, None, :]
    onehot = hit.any(axis=1)
    blocks = onehot.reshape(n_tok // blk, blk, n_experts).astype(BF16)
    tri = jnp.asarray(np.tril(np.ones((blk, blk), np.float32), -1), BF16)
    within = jnp.einsum("ij,bjk->bik", tri, blocks, preferred_element_type=F32).astype(jnp.int32)
    block_tot = onehot.reshape(n_tok // blk, blk, n_experts).sum(axis=1, dtype=jnp.int32)
    block_pre = jnp.cumsum(block_tot, axis=0) - block_tot
    pos = (within + block_pre[:, None, :]).reshape(n_tok, n_experts)
    counts = block_tot.sum(axis=0)
    padded = ((counts + tile - 1) // tile) * tile
    ends = jnp.cumsum(padded)
    offs = ends - padded
    slots = jnp.sum(jnp.where(hit, (offs[None, :] + pos)[:, None, :], 0), axis=2)
    n_tiles = (n_tok * TOP_K) // tile + n_experts
    tile_start = jnp.arange(n_tiles, dtype=jnp.int32) * tile
    tile_expert = jnp.minimum(jnp.sum(tile_start[:, None] >= ends[None, :], axis=1), n_experts - 1).astype(jnp.int32)
    n_used = (ends[-1] // tile).astype(jnp.int32)
    last = tile_expert[jnp.maximum(n_used - 1, 0)]
    tile_expert = jnp.where(jnp.arange(n_tiles) < n_used, tile_expert, last)
    return slots.astype(jnp.int32).reshape(-1), tile_expert, n_used.reshape(1), n_tiles * tile


def _rope_tables(n, grid_w, extra_rows):
    t = np.arange(n)
    row = (t // grid_w).astype(np.float32)[:, None]
    col = (t % grid_w).astype(np.float32)[:, None]
    axis_dim = HEAD_DIM // 2
    inv_freq = (ROPE_THETA ** (-np.arange(0, axis_dim, 2, dtype=np.float32) / axis_dim)).astype(np.float32)
    ang_r = row * inv_freq
    ang_c = col * inv_freq
    ang = np.concatenate([ang_r, ang_r, ang_c, ang_c], axis=-1)
    cos = np.cos(ang).astype(np.float32)
    sin = np.sin(ang).astype(np.float32)
    sign = np.where((np.arange(HEAD_DIM) % 32) < 16, -1.0, 1.0).astype(np.float32)
    sin = sin * sign[None, :]
    cos = np.concatenate([cos, np.ones((extra_rows, HEAD_DIM), np.float32)], axis=0)
    sin = np.concatenate([sin, np.zeros((extra_rows, HEAD_DIM), np.float32)], axis=0)
    reps = LANES // HEAD_DIM
    return jnp.asarray(np.tile(cos, (1, reps))), jnp.asarray(np.tile(sin, (1, reps)))


def _forward(x, c, ctx, c_ctx, ada_w, ada_b, w_in, w_out, q_gain, k_gain, hy_conv_w, hy_conv_b,
             hy_w1, hy_b1, hy_freq, hy_w2, hy_b2, hy_w3, hy_b3, hy_d, na_rpb, ln1_g, ln1_b,
             ln2_g, ln2_b, router_w, router_b, exp_w1, exp_b1, exp_w2, exp_b2, *, grid_w):
    nb, n, d = x.shape
    m = ctx.shape[1]
    depth = ada_w.shape[0]
    n_experts = router_w.shape[-1]
    alpha = (2.0 * depth) ** 0.25
    tm = TOKEN_TILE
    assert n % tm == 0 and m % tm == 0 and n % m == 0 and n_experts <= LANES
    t_lat, t_all = nb * n, nb * (n + m)

    xf = jnp.concatenate([x.reshape(nb * n, d), ctx.reshape(nb * m, d)], axis=0)
    cvec = jnp.zeros((SUBLANES, d), F32).at[:nb].set(c).at[nb].set(c_ctx)
    mods = _adaln(cvec, ada_w, ada_b).reshape(depth, SUBLANES, 6, d)

    tiles = np.arange(t_all // tm)
    lat_tile = tiles < t_lat // tm
    tile_mod = jnp.asarray(np.where(lat_tile, tiles // (n // tm), nb), jnp.int32)
    tile_pos = jnp.asarray(np.where(lat_tile, tiles % (n // tm), n // tm), jnp.int32)
    cos_t, sin_t = _rope_tables(n, grid_w, tm)
    gmat = jnp.asarray(np.kron(np.eye(ATT_Q_HEADS), np.ones((HEAD_DIM, HEAD_DIM))), BF16)
    dft_lat = _dft_constants(n)
    dft_ctx = _dft_constants(m)

    for l in range(depth):
        need_ctx = l < depth - 1
        n_tok = t_all if need_ctx else t_lat
        q, k, v, hyp, nq, nk, nv = _proj(
            xf, mods[l], w_in[l].astype(BF16), jnp.tile(q_gain[l], ATT_Q_HEADS)[None, :],
            jnp.tile(k_gain[l], ATT_KV_HEADS)[None, :], gmat, cos_t, sin_t, tile_mod, tile_pos)
        y_att = _gqa(q, k, v, nb, n, m, need_ctx)
        filt = (hy_w1[l], hy_b1[l], hy_freq[l], hy_w2[l], hy_b2[l], hy_w3[l], hy_b3[l])
        gre, gim = _hyena_filter_spectrum(n, dft_lat, *filt)
        cb = hy_conv_b[l][None, :]
        dsk = hy_d[l][None, :]
        y_hy = _hyena(hyp, hy_conv_w[l], cb, dsk, gre, gim, dft_lat, nb, n, 0)
        if need_ctx:
            gre_c, gim_c = _hyena_filter_spectrum(m, dft_ctx, *filt)
            y_hy_c = _hyena(hyp, hy_conv_w[l], cb, dsk, gre_c, gim_c, dft_ctx, nb, m, t_lat // m)
            y_hy = jnp.concatenate([y_hy, y_hy_c], axis=0)
        bias_tab = _na_bias_table(na_rpb[l], grid_w, n // grid_w)
        y_na = _na(nq, nk, nv, bias_tab, nb, n, m, grid_w, need_ctx)

        router_wp = jnp.zeros((d, LANES), F32).at[:, :n_experts].set(router_w[l])
        router_bp = jnp.zeros((1, LANES), F32).at[0, :n_experts].set(router_b[l])
        x1, h2p, idx, wts = _mix_out(xf, y_att, y_hy, y_na, mods[l], w_out[l].astype(BF16), ln1_g[l][None, :],
                                     ln1_b[l][None, :], router_wp, router_bp, tile_mod, n_tok, alpha, n_experts)
        slots, tile_expert, n_used, n_slots = _routing_tables(idx[:, :TOP_K], n_experts, MOE_TILE)
        xs = _dispatch(h2p, slots, n_slots)
        ys = _experts(xs, tile_expert, n_used, exp_w1.reshape((-1,) + exp_w1.shape[2:]), exp_b1[l],
                      exp_w2.reshape((-1,) + exp_w2.shape[2:]), exp_b2[l], l)
        xf = _combine(x1, wts, slots, ys, mods[l], ln2_g[l][None, :], ln2_b[l][None, :], tile_mod, alpha)
    return xf[:t_lat].reshape(nb, n, d)


def kernel(x, c, ctx, c_ctx, ada_w, ada_b, w_in, w_out, q_gain, k_gain, hy_conv_w, hy_conv_b, hy_w1, hy_b1,
           hy_freq, hy_w2, hy_b2, hy_w3, hy_b3, hy_d, na_rpb, ln1_g, ln1_b, ln2_g, ln2_b, router_w, router_b,
           exp_w1, exp_b1, exp_w2, exp_b2):
    return _forward(x, c, ctx, c_ctx, ada_w, ada_b, w_in, w_out, q_gain, k_gain, hy_conv_w, hy_conv_b, hy_w1,
                    hy_b1, hy_freq, hy_w2, hy_b2, hy_w3, hy_b3, hy_d, na_rpb, ln1_g, ln1_b, ln2_g, ln2_b,
                    router_w, router_b, exp_w1, exp_b1, exp_w2, exp_b2, grid_w=64)
```

```python
import cmath
import functools
import math

import jax
import jax.numpy as jnp
import numpy as np
from jax import lax
from jax.experimental import pallas as pl
from jax.experimental.pallas import tpu as pltpu

F32 = jnp.float32
BF16 = jnp.bfloat16
HIGHEST = lax.Precision.HIGHEST

HEAD_DIM = 64
ATT_Q_HEADS = 8
ATT_KV_HEADS = 2
ATT_GROUP = ATT_Q_HEADS // ATT_KV_HEADS
ATT_W = ATT_Q_HEADS * HEAD_DIM
KV_W = ATT_KV_HEADS * HEAD_DIM
HY_W = 256
HY_SHORT = 3
HY_EMB = 33
HY_BANDS = (HY_EMB - 1) // 2
HY_FO = 64
HY_MIN_DECAY = math.log(1e-2) / 1.5
HY_MAX_DECAY = math.log(1e-2) / 0.3
NA_HEADS = 4
NA_W = NA_HEADS * HEAD_DIM
NA_WIN_ROWS = 8
NA_WIN_COLS = 16
D_MIX = ATT_W + HY_W + NA_W
D_IN = ATT_W + 2 * KV_W + 3 * HY_W + 3 * NA_W
TOP_K = 4
SWIGLU_ALPHA = 1.702
SWIGLU_LIMIT = 7.0
ROPE_THETA = 10000.0
NORM_EPS = 1e-6
Q_SCALE = HEAD_DIM ** -0.5

LANES = 128
SUBLANES = 8
NEG_BIG = -1e30

TOKEN_TILE = 256
ATT_Q_TILE = 256
NA_Q_ROWS = 8
NA_K_ROWS = 16
FFT_N1 = 8
HY_CB = 128
FREQ_CHUNK = 256
MOE_TILE = 512
VMEM_LIMIT = 56 * 1024 * 1024


def _cparams(sem, vmem=VMEM_LIMIT):
    return pltpu.CompilerParams(dimension_semantics=sem, vmem_limit_bytes=vmem)


def _layer_norm(x):
    mu = jnp.mean(x, axis=-1, keepdims=True)
    xc = x - mu
    var = jnp.mean(xc * xc, axis=-1, keepdims=True)
    return xc * lax.rsqrt(var + NORM_EPS)


def _dot(a, b):
    return jnp.dot(a, b, preferred_element_type=F32)


def _dot_nt(a, b):
    return lax.dot_general(a, b, (((1,), (1,)), ((), ())), preferred_element_type=F32)


def _split_bf16(x):
    hi = x.astype(BF16)
    lo = (x - hi.astype(F32)).astype(BF16)
    return hi, lo


def _dot3(a_hi, a_lo, b_hi, b_lo):
    return _dot(a_hi, b_hi) + _dot(a_hi, b_lo) + _dot(a_lo, b_hi)


def _adaln_kernel(c_ref, w_ref, b_ref, o_ref):
    c = c_ref[...]
    a = c * jax.nn.sigmoid(c)
    o_ref[0] = jnp.dot(a, w_ref[0], preferred_element_type=F32, precision=HIGHEST) + b_ref[0]


def _adaln(cvec, ada_w, ada_b):
    n_layers, d, d6 = ada_w.shape
    bn = d6 // 4 if (d6 // 4) % LANES == 0 else d6
    return pl.pallas_call(
        _adaln_kernel,
        grid=(n_layers, d6 // bn),
        in_specs=[
            pl.BlockSpec((SUBLANES, d), lambda l, j: (0, 0)),
            pl.BlockSpec((1, d, bn), lambda l, j: (l, 0, j)),
            pl.BlockSpec((1, 1, bn), lambda l, j: (l, 0, j)),
        ],
        out_specs=pl.BlockSpec((1, SUBLANES, bn), lambda l, j: (l, 0, j)),
        out_shape=jax.ShapeDtypeStruct((n_layers, SUBLANES, d6), F32),
        compiler_params=_cparams(("arbitrary", "arbitrary")),
        name="adaln",
    )(cvec, ada_w, ada_b.reshape(n_layers, 1, d6))


def _head_norm(a, g_ref, gain):
    w = a.shape[1]
    sq = a * a
    hi, lo = _split_bf16(sq)
    g = g_ref[0:w, 0:w]
    ss = _dot(hi, g) + _dot(lo, g)
    return a * lax.rsqrt(ss * (1.0 / HEAD_DIM) + NORM_EPS) * gain


def _rope(a, cos, sin_signed):
    w = a.shape[1]
    reps = w // LANES
    c = jnp.concatenate([cos] * reps, axis=1) if reps > 1 else cos
    s = jnp.concatenate([sin_signed] * reps, axis=1) if reps > 1 else sin_signed
    lane = lax.broadcasted_iota(jnp.int32, a.shape, 1)
    first = (lane & 31) < 16
    rot = jnp.where(first, pltpu.roll(a, w - 16, axis=1), pltpu.roll(a, 16, axis=1))
    return a * c + rot * s


def _proj_kernel(tmod_ref, tpos_ref, x_ref, mod_ref, w_ref, qg_ref, kg_ref, g_ref, cos_ref, sin_ref,
                 q_out, k_out, v_out, hy_out, nq_out, nk_out, nv_out):
    del tmod_ref, tpos_ref
    h = _layer_norm(x_ref[...]) * (1.0 + mod_ref[0, 1:2, :]) + mod_ref[0, 0:1, :]
    p = _dot(h.astype(BF16), w_ref[...])
    o = 0
    aq = p[:, o:o + ATT_W]; o += ATT_W
    ak = p[:, o:o + KV_W]; o += KV_W
    av = p[:, o:o + KV_W]; o += KV_W
    hy = p[:, o:o + 3 * HY_W]; o += 3 * HY_W
    nq = p[:, o:o + NA_W]; o += NA_W
    nk = p[:, o:o + NA_W]; o += NA_W
    nv = p[:, o:o + NA_W]
    cos = cos_ref[...]
    sin = sin_ref[...]
    q = _rope(_head_norm(aq, g_ref, qg_ref[...]), cos, sin) * Q_SCALE
    k = _rope(_head_norm(ak, g_ref, kg_ref[...]), cos, sin)
    q_out[...] = q.astype(BF16)
    k_out[...] = k.astype(BF16)
    v_out[...] = av.astype(BF16)
    hy_out[...] = hy
    nq_out[...] = (nq * Q_SCALE).astype(BF16)
    nk_out[...] = nk.astype(BF16)
    nv_out[...] = nv.astype(BF16)


def _proj(xf, mods_l, w_in_bf, q_gain, k_gain, gmat, cos_t, sin_t, tile_mod, tile_pos):
    t, d = xf.shape
    tm = TOKEN_TILE
    n_tiles = t // tm
    row = lambda i, tmod, tpos: (i, 0)
    const = lambda i, tmod, tpos: (0, 0)
    widths = (ATT_W, KV_W, KV_W, 3 * HY_W, NA_W, NA_W, NA_W)
    dtypes = (BF16, BF16, BF16, F32, BF16, BF16, BF16)
    grid_spec = pltpu.PrefetchScalarGridSpec(
        num_scalar_prefetch=2,
        grid=(n_tiles,),
        in_specs=[
            pl.BlockSpec((tm, d), row),
            pl.BlockSpec((1, 6, d), lambda i, tmod, tpos: (tmod[i], 0, 0)),
            pl.BlockSpec((d, D_IN), const),
            pl.BlockSpec((1, ATT_W), const),
            pl.BlockSpec((1, KV_W), const),
            pl.BlockSpec((ATT_W, ATT_W), const),
            pl.BlockSpec((tm, LANES), lambda i, tmod, tpos: (tpos[i], 0)),
            pl.BlockSpec((tm, LANES), lambda i, tmod, tpos: (tpos[i], 0)),
        ],
        out_specs=[pl.BlockSpec((tm, w), row) for w in widths],
    )
    return pl.pallas_call(
        _proj_kernel,
        grid_spec=grid_spec,
        out_shape=[jax.ShapeDtypeStruct((t, w), dt) for w, dt in zip(widths, dtypes)],
        compiler_params=_cparams(("arbitrary",)),
        name="ln_mod_proj",
    )(tile_mod, tile_pos, xf, mods_l, w_in_bf, q_gain, k_gain, gmat, cos_t, sin_t)


def _attend(q, segs, n_kv, grp, bias=None):
    tq = q.shape[0]
    hd = HEAD_DIM
    outs = []
    for j in range(n_kv):
        heads = [q[:, (j * grp + g) * hd:(j * grp + g + 1) * hd] for g in range(grp)]
        qs = jnp.concatenate(heads, axis=0) if grp > 1 else heads[0]
        scores = []
        for si, (k, _) in enumerate(segs):
            s = _dot_nt(qs, k[:, j * hd:(j + 1) * hd])
            if bias is not None and bias[j][si] is not None:
                s = s + bias[j][si]
            scores.append(s)
        m = scores[0].max(axis=-1, keepdims=True)
        for s in scores[1:]:
            m = jnp.maximum(m, s.max(axis=-1, keepdims=True))
        denom = None
        acc = None
        for s, (_, v) in zip(scores, segs):
            p = jnp.exp(s - m)
            ps = p.sum(axis=-1, keepdims=True)
            pv = _dot(p.astype(BF16), v[:, j * hd:(j + 1) * hd])
            denom = ps if denom is None else denom + ps
            acc = pv if acc is None else acc + pv
        o = acc / denom
        for g in range(grp):
            outs.append(o[g * tq:(g + 1) * tq])
    return jnp.concatenate(outs, axis=1)


def _gqa_kernel(q_ref, kl_ref, vl_ref, kc_ref, vc_ref, o_ref):
    segs = [(kl_ref[...], vl_ref[...]), (kc_ref[...], vc_ref[...])]
    o_ref[...] = _attend(q_ref[...], segs, ATT_KV_HEADS, ATT_GROUP).astype(o_ref.dtype)


def _gqa(q, k, v, nb, n, m, with_ctx):
    t = q.shape[0]
    tq = ATT_Q_TILE
    n_lat = n // tq
    ctx_blk0 = nb * n // m
    qmap = lambda b, i: (b * n_lat + i, 0)
    lat = lambda b, i: (b, 0)
    ctx = lambda b, i: (ctx_blk0 + b, 0)
    y = pl.pallas_call(
        _gqa_kernel,
        grid=(nb, n_lat),
        in_specs=[
            pl.BlockSpec((tq, ATT_W), qmap),
            pl.BlockSpec((n, KV_W), lat),
            pl.BlockSpec((n, KV_W), lat),
            pl.BlockSpec((m, KV_W), ctx),
            pl.BlockSpec((m, KV_W), ctx),
        ],
        out_specs=pl.BlockSpec((tq, ATT_W), qmap),
        out_shape=jax.ShapeDtypeStruct((nb * n, ATT_W), BF16),
        compiler_params=_cparams(("arbitrary", "arbitrary")),
        name="gqa_attention",
    )(q, k, v, k, v)
    if not with_ctx:
        return y
    yc = _ctx_attention(q, k, v, nb, n, m, ATT_KV_HEADS, ATT_GROUP, ATT_W, "gqa_ctx_attention")
    return jnp.concatenate([y, yc], axis=0)


def _na_kernel(q_ref, kl_ref, vl_ref, kc_ref, vc_ref, bias_ref, o_ref, *, grid_w, grid_rows):
    i = pl.program_id(1)
    krow0 = jnp.clip(i * NA_Q_ROWS - NA_WIN_ROWS // 2, 0, grid_rows - NA_K_ROWS)
    start = pl.multiple_of(krow0 * grid_w, 4 * grid_w)
    nk = NA_K_ROWS * grid_w
    segs = [(kl_ref[pl.ds(start, nk), :], vl_ref[pl.ds(start, nk), :]), (kc_ref[...], vc_ref[...])]
    bias = [[bias_ref[0, h], None] for h in range(NA_HEADS)]
    o_ref[...] = _attend(q_ref[...], segs, NA_HEADS, 1, bias).astype(o_ref.dtype)


def _na_bias_table(rpb, grid_w, grid_rows):
    qr, kr = NA_Q_ROWS, NA_K_ROWS
    n_rr, n_rc = 2 * NA_WIN_ROWS - 1, 2 * NA_WIN_COLS - 1
    c = np.arange(grid_w)[:, None]
    kcol = np.arange(grid_w)[None, :]
    cs = np.clip(c - NA_WIN_COLS // 2, 0, grid_w - NA_WIN_COLS)
    col_ok = (kcol >= cs) & (kcol < cs + NA_WIN_COLS)
    csel = ((kcol - c + NA_WIN_COLS - 1)[:, :, None] == np.arange(n_rc)) & col_ok[:, :, None]
    rsel, mask = [], []
    for r0 in (0, qr, grid_rows - qr):
        k0 = int(np.clip(r0 - NA_WIN_ROWS // 2, 0, grid_rows - kr))
        r = r0 + np.arange(qr)[:, None]
        krow = k0 + np.arange(kr)[None, :]
        rs = np.clip(r - NA_WIN_ROWS // 2, 0, grid_rows - NA_WIN_ROWS)
        row_ok = (krow >= rs) & (krow < rs + NA_WIN_ROWS)
        rsel.append(((krow - r + NA_WIN_ROWS - 1)[:, :, None] == np.arange(n_rr)) & row_ok[:, :, None])
        ok = row_ok[:, None, :, None] & col_ok[None, :, None, :]
        mask.append(np.where(ok, 0.0, NEG_BIG))
    rsel = jnp.asarray(np.stack(rsel), F32)
    mask = jnp.asarray(np.stack(mask), F32)
    vals = jnp.einsum("tjia,hab,ckb->thjcik", rsel, rpb, jnp.asarray(csel, F32), precision=HIGHEST)
    return (vals + mask[:, None]).reshape(3, NA_HEADS, qr * grid_w, kr * grid_w)


def _na(nq, nk, nv, bias_tab, nb, n, m, grid_w, with_ctx):
    t = nq.shape[0]
    grid_rows = n // grid_w
    tq = NA_Q_ROWS * grid_w
    n_lat = n // tq
    ctx_blk0 = nb * n // m
    lat = lambda b, i: (b, 0)
    ctx = lambda b, i: (ctx_blk0 + b, 0)
    qmap = lambda b, i: (b * n_lat + i, 0)

    def bmap(b, i):
        return (jnp.where(i == 0, 0, jnp.where(i >= n_lat - 1, 2, 1)), 0, 0, 0)

    y = pl.pallas_call(
        functools.partial(_na_kernel, grid_w=grid_w, grid_rows=grid_rows),
        grid=(nb, n_lat),
        in_specs=[
            pl.BlockSpec((tq, NA_W), qmap),
            pl.BlockSpec((n, NA_W), lat),
            pl.BlockSpec((n, NA_W), lat),
            pl.BlockSpec((m, NA_W), ctx),
            pl.BlockSpec((m, NA_W), ctx),
            pl.BlockSpec((1, NA_HEADS, tq, NA_K_ROWS * grid_w), bmap),
        ],
        out_specs=pl.BlockSpec((tq, NA_W), qmap),
        out_shape=jax.ShapeDtypeStruct((nb * n, NA_W), BF16),
        compiler_params=_cparams(("arbitrary", "arbitrary")),
        name="neighbourhood_attention",
    )(nq, nk, nv, nk, nv, bias_tab)
    if not with_ctx:
        return y
    yc = _ctx_attention(nq, nk, nv, nb, n, m, NA_HEADS, 1, NA_W, "na_ctx_attention")
    return jnp.concatenate([y, yc], axis=0)


def _ctx_attn_kernel(q_ref, k_ref, v_ref, o_ref, *, n_kv, grp):
    segs = [(k_ref[...], v_ref[...])]
    o_ref[...] = _attend(q_ref[...], segs, n_kv, grp).astype(o_ref.dtype)


def _ctx_attention(q, k, v, nb, n, m, n_kv, grp, width, name):
    ctx_blk0 = nb * n // m
    ctx = lambda b: (ctx_blk0 + b, 0)
    kw = n_kv * HEAD_DIM
    return pl.pallas_call(
        functools.partial(_ctx_attn_kernel, n_kv=n_kv, grp=grp),
        grid=(nb,),
        in_specs=[
            pl.BlockSpec((m, width), ctx),
            pl.BlockSpec((m, kw), ctx),
            pl.BlockSpec((m, kw), ctx),
        ],
        out_specs=pl.BlockSpec((m, width), lambda b: (b, 0)),
        out_shape=jax.ShapeDtypeStruct((nb * m, width), BF16),
        compiler_params=_cparams(("arbitrary",)),
        name=name,
    )(q, k, v)


def _dft_constants(seq_len):
    n = 2 * seq_len
    n2 = n // FFT_N1
    idx = np.arange(n2, dtype=np.int64)
    ang = 2.0 * np.pi * ((idx[:, None] * idx[None, :]) % n2).astype(np.float64) / n2
    cmat, smat = np.cos(ang), np.sin(ang)

    def split(a):
        hi = jnp.asarray(a, F32).astype(BF16)
        lo = (jnp.asarray(a, F32) - hi.astype(F32)).astype(BF16)
        return hi, lo

    def col_chunks(a):
        return np.transpose(a.reshape(a.shape[0], n2 // chunk, chunk), (1, 0, 2))

    chunk = min(FREQ_CHUNK, n2)
    tw_ang = 2.0 * np.pi * (idx[:, None] * np.arange(FFT_N1)[None, :]).astype(np.float64) / n
    return dict(
        n2=n2, chunk=chunk,
        fwd_c=split(cmat), fwd_s=split(smat),
        inv_c=split(col_chunks(cmat[: n2 // 2] / n)),
        inv_s=split(col_chunks(smat[: n2 // 2] / n)),
        tw_re=jnp.asarray(np.cos(tw_ang), F32), tw_im=jnp.asarray(-np.sin(tw_ang), F32),
    )


def _cmul_const(a, w):
    re, im = a
    if abs(w - 1) < 1e-12:
        return a
    if abs(w + 1) < 1e-12:
        return (-re, -im)
    if abs(w + 1j) < 1e-12:
        return (im, -re)
    if abs(w - 1j) < 1e-12:
        return (-im, re)
    return (re * w.real - im * w.imag, re * w.imag + im * w.real)


def _fft_blocks(xs, sign):
    n = len(xs)
    if n == 1:
        return xs
    even = _fft_blocks(xs[0::2], sign)
    odd = _fft_blocks(xs[1::2], sign)
    out = [None] * n
    for k in range(n // 2):
        t = _cmul_const(odd[k], cmath.exp(sign * 2j * cmath.pi * k / n))
        out[k] = (even[k][0] + t[0], even[k][1] + t[1])
        out[k + n // 2] = (even[k][0] - t[0], even[k][1] - t[1])
    return out


def _lane_blocks(a):
    return [a[:, j * LANES:(j + 1) * LANES] for j in range(a.shape[1] // LANES)]


def _forward_spectrum(xh, xl, rows, ch_ref, cl_ref, sh_ref, sl_ref, twr_ref, twi_ref):
    a_re = _dot3(ch_ref[rows, :], cl_ref[rows, :], xh, xl)
    a_im = -_dot3(sh_ref[rows, :], sl_ref[rows, :], xh, xl)
    blocks = []
    for s1, (re, im) in enumerate(zip(_lane_blocks(a_re), _lane_blocks(a_im))):
        tr = twr_ref[rows, s1:s1 + 1]
        ti = twi_ref[rows, s1:s1 + 1]
        blocks.append((re * tr - im * ti, re * ti + im * tr))
    return _fft_blocks(blocks, -1)


def _filter_kernel(emb_ref, w1_ref, b1_ref, fr_ref, w2_ref, b2_ref, w3f_ref, w3b_ref, b3f_ref, b3b_ref,
                   dec_ref, ch_ref, cl_ref, sh_ref, sl_ref, twr_ref, twi_ref, gre_ref, gim_ref, g_scr, hid_scr, *,
                   seq_len, chunk):
    n = 2 * seq_len
    n2 = n // FFT_N1
    def mm(a, w_ref):
        return _dot3(*_split_bf16(a), *_split_bf16(w_ref[...]))

    @pl.when(pl.program_id(0) == 0)
    def _():
        hid1 = jnp.sin(fr_ref[0:1, :] * (mm(emb_ref[...], w1_ref) + b1_ref[...]))
        hid_scr[...] = jnp.sin(fr_ref[1:2, :] * (mm(hid1, w2_ref) + b2_ref[...]))

    hid = hid_scr[...]
    f_fwd = mm(hid, w3f_ref) + b3f_ref[...]
    f_bwd = mm(hid, w3b_ref) + b3b_ref[...]
    row = lax.broadcasted_iota(jnp.int32, f_fwd.shape, 0)
    t_pos = emb_ref[:, 0:1]
    g = jnp.where(row < seq_len, f_fwd, jnp.where(row > seq_len, f_bwd, 0.0))
    g = g * jnp.exp(-t_pos * jnp.abs(dec_ref[...]))
    g = g * lax.rsqrt(jnp.sum(g * g, axis=0, keepdims=True) + NORM_EPS)
    g_scr[...] = g
    x = jnp.concatenate([g_scr[pl.ds(s1, n2, stride=FFT_N1), :] for s1 in range(FFT_N1)], axis=1)
    xh, xl = _split_bf16(x)

    def freq_chunk(ci, carry):
        rows = pl.ds(pl.multiple_of(ci * chunk, chunk), chunk)
        spec = _forward_spectrum(xh, xl, rows, ch_ref, cl_ref, sh_ref, sl_ref, twr_ref, twi_ref)
        gre_ref[0, rows, :] = jnp.concatenate([b[0] for b in spec], axis=1)
        gim_ref[0, rows, :] = jnp.concatenate([b[1] for b in spec], axis=1)
        return carry

    lax.fori_loop(0, n2 // chunk, freq_chunk, 0)


def _filter_embedding(seq_len):
    n = 2 * seq_len
    pos = np.concatenate([np.arange(seq_len), [0], np.arange(seq_len - 1, 0, -1)])
    t = np.linspace(0.0, 1.0, seq_len, dtype=np.float32)[pos][:, None]
    w = ((2.0 * math.pi / seq_len) * np.arange(seq_len, dtype=np.float32))[pos][:, None]
    bands = np.linspace(1e-4, HY_BANDS - 1, HY_BANDS, dtype=np.float32)[None, :]
    z = np.concatenate([t, np.cos(bands * w), -np.sin(bands * w)], axis=-1).astype(np.float32)
    emb = np.zeros((n, LANES), np.float32)
    emb[:, :HY_EMB] = z
    return jnp.asarray(emb)


def _hyena_filter_spectrum(seq_len, consts, f_w1, f_b1, f_freq, f_w2, f_b2, f_w3, f_b3):
    n = 2 * seq_len
    n2 = consts["n2"]
    ncb = HY_W // HY_CB
    emb = _filter_embedding(seq_len)
    w1p = jnp.zeros((LANES, HY_FO), F32).at[:HY_EMB].set(f_w1)
    deltas = jnp.asarray(np.linspace(HY_MIN_DECAY, HY_MAX_DECAY, HY_W, dtype=np.float32))[None, :]
    const = lambda j: (0, 0)
    out_sds = jax.ShapeDtypeStruct((ncb, n2, FFT_N1 * HY_CB), F32)
    return pl.pallas_call(
        functools.partial(_filter_kernel, seq_len=seq_len, chunk=consts["chunk"]),
        grid=(ncb,),
        in_specs=[
            pl.BlockSpec((n, LANES), const),
            pl.BlockSpec((LANES, HY_FO), const),
            pl.BlockSpec((1, HY_FO), const),
            pl.BlockSpec((2, HY_FO), const),
            pl.BlockSpec((HY_FO, HY_FO), const),
            pl.BlockSpec((1, HY_FO), const),
            pl.BlockSpec((HY_FO, HY_CB), lambda j: (0, j)),
            pl.BlockSpec((HY_FO, HY_CB), lambda j: (0, ncb + j)),
            pl.BlockSpec((1, HY_CB), lambda j: (0, j)),
            pl.BlockSpec((1, HY_CB), lambda j: (0, ncb + j)),
            pl.BlockSpec((1, HY_CB), lambda j: (0, j)),
        ] + [pl.BlockSpec((n2, n2), const, pipeline_mode=pl.Buffered(1))] * 4
          + [pl.BlockSpec((n2, FFT_N1), const, pipeline_mode=pl.Buffered(1))] * 2,
        out_specs=[pl.BlockSpec((1, n2, FFT_N1 * HY_CB), lambda j: (j, 0, 0))] * 2,
        out_shape=[out_sds, out_sds],
        scratch_shapes=[pltpu.VMEM((n, HY_CB), F32), pltpu.VMEM((n, HY_FO), F32)],
        compiler_params=_cparams(("arbitrary",)),
        name=f"hyena_filter_{seq_len}",
    )(emb, w1p, f_b1[None, :], f_freq, f_w2, f_b2[None, :], f_w3, f_w3, f_b3[None, :], f_b3[None, :], deltas,
      *consts["fwd_c"], *consts["fwd_s"], consts["tw_re"], consts["tw_im"])


def _short_conv(p, w_ref, b_ref):
    n = p.shape[0]
    row = lax.broadcasted_iota(jnp.int32, p.shape, 0)
    prev = jnp.where(row == 0, 0.0, pltpu.roll(p, 1, axis=0))
    nxt = jnp.where(row == n - 1, 0.0, pltpu.roll(p, n - 1, axis=0))
    return prev * w_ref[0:1, :] + p * w_ref[1:2, :] + nxt * w_ref[2:3, :] + b_ref[...]


def _hyena_kernel(p0_ref, p1_ref, p2_ref, w0_ref, w1_ref, w2_ref, b0_ref, b1_ref, b2_ref, d_ref,
                  gre_ref, gim_ref, fch_ref, fcl_ref, fsh_ref, fsl_ref, ich_ref, icl_ref, ish_ref, isl_ref,
                  twr_ref, twi_ref, o_ref, z_scr, y_scr, acc_scr, *, seq_len, chunk):
    n2 = 2 * seq_len // FFT_N1
    half = n2 // 2
    x1 = _short_conv(p1_ref[...], w1_ref, b1_ref)
    v = _short_conv(p2_ref[...], w2_ref, b2_ref)
    z_scr[...] = x1 * v
    x = jnp.concatenate([z_scr[pl.ds(s1, half, stride=FFT_N1), :] for s1 in range(FFT_N1)], axis=1)
    xh, xl = _split_bf16(x)
    acc_scr[...] = jnp.zeros(acc_scr.shape, F32)

    def freq_chunk(ci, carry):
        rows = pl.ds(pl.multiple_of(ci * chunk, chunk), chunk)
        zf = _forward_spectrum(xh, xl, rows, fch_ref, fcl_ref, fsh_ref, fsl_ref, twr_ref, twi_ref)
        gre = _lane_blocks(gre_ref[0, rows, :])
        gim = _lane_blocks(gim_ref[0, rows, :])
        prod = [(zr * gr - zi * gi, zr * gi + zi * gr) for (zr, zi), gr, gi in zip(zf, gre, gim)]
        back = _fft_blocks(prod, +1)
        c_re, c_im = [], []
        for s1, (re, im) in enumerate(back):
            tr = twr_ref[rows, s1:s1 + 1]
            ti = -twi_ref[rows, s1:s1 + 1]
            c_re.append(re * tr - im * ti)
            c_im.append(re * ti + im * tr)
        rh, rl = _split_bf16(jnp.concatenate(c_re, axis=1))
        ih, il = _split_bf16(jnp.concatenate(c_im, axis=1))
        acc_scr[...] += _dot3(ich_ref[ci], icl_ref[ci], rh, rl) - _dot3(ish_ref[ci], isl_ref[ci], ih, il)
        return carry

    lax.fori_loop(0, n2 // chunk, freq_chunk, 0)
    for s1 in range(FFT_N1):
        y_scr[pl.ds(s1, half, stride=FFT_N1), :] = acc_scr[:, s1 * LANES:(s1 + 1) * LANES]
    x0 = _short_conv(p0_ref[...], w0_ref, b0_ref)
    o_ref[...] = (x0 * (y_scr[...] + z_scr[...] * d_ref[...])).astype(o_ref.dtype)


def _hyena(hyp, conv_w, conv_b, d_skip, gre, gim, consts, nb, seq_len, row_blk0):
    n2 = consts["n2"]
    chunk = consts["chunk"]
    ncb = HY_W // HY_CB
    half = n2 // 2
    const = lambda j, b: (0, 0)
    once = pl.Buffered(1)
    in_specs = (
        [pl.BlockSpec((seq_len, HY_CB), (lambda j, b, g=g: (row_blk0 + b, g * ncb + j))) for g in range(3)]
        + [pl.BlockSpec((HY_SHORT, HY_CB), (lambda j, b, g=g: (0, g * ncb + j))) for g in range(3)]
        + [pl.BlockSpec((1, HY_CB), (lambda j, b, g=g: (0, g * ncb + j))) for g in range(3)]
        + [pl.BlockSpec((1, HY_CB), lambda j, b: (0, j))]
        + [pl.BlockSpec((1, n2, FFT_N1 * HY_CB), lambda j, b: (j, 0, 0), pipeline_mode=once)] * 2
        + [pl.BlockSpec((n2, half), const, pipeline_mode=once)] * 4
        + [pl.BlockSpec((n2 // chunk, half, chunk), lambda j, b: (0, 0, 0), pipeline_mode=once)] * 4
        + [pl.BlockSpec((n2, FFT_N1), const, pipeline_mode=once)] * 2
    )
    fwd = [a[:, :half] for a in (*consts["fwd_c"], *consts["fwd_s"])]
    args = [hyp, hyp, hyp, conv_w, conv_w, conv_w, conv_b, conv_b, conv_b, d_skip, gre, gim,
            *fwd, *consts["inv_c"], *consts["inv_s"], consts["tw_re"], consts["tw_im"]]
    return pl.pallas_call(
        functools.partial(_hyena_kernel, seq_len=seq_len, chunk=chunk),
        grid=(ncb, nb),
        in_specs=in_specs,
        out_specs=pl.BlockSpec((seq_len, HY_CB), lambda j, b: (b, j)),
        out_shape=jax.ShapeDtypeStruct((nb * seq_len, HY_W), BF16),
        scratch_shapes=[pltpu.VMEM((seq_len, HY_CB), F32), pltpu.VMEM((seq_len, HY_CB), F32),
                        pltpu.VMEM((half, FFT_N1 * HY_CB), F32)],
        compiler_params=_cparams(("arbitrary", "arbitrary")),
        name=f"hyena_mixer_{seq_len}",
    )(*args)


def _pack_pairs(h):
    half = h.shape[1] // 2
    lo = pltpu.bitcast(h[:, :half].astype(BF16).astype(F32), jnp.uint32) >> 16
    hi = pltpu.bitcast(h[:, half:].astype(BF16).astype(F32), jnp.uint32) & jnp.uint32(0xFFFF0000)
    return lo | hi


def _unpack_pairs(u):
    lo = pltpu.bitcast(u << 16, F32)
    hi = pltpu.bitcast(u & jnp.uint32(0xFFFF0000), F32)
    return jnp.concatenate([lo, hi], axis=1).astype(BF16)


def _mix_out_kernel(tmod_ref, x_ref, ya_ref, yh_ref, yn_ref, mod_ref, w_ref, g_ref, b_ref, rwh_ref, rwl_ref, rb_ref,
                    x1_out, h2_out, idx_out, wt_out, *, alpha, n_experts):
    del tmod_ref
    y = _dot(jnp.concatenate([ya_ref[...], yh_ref[...], yn_ref[...]], axis=1), w_ref[...])
    x1 = _layer_norm(alpha * x_ref[...] + mod_ref[0, 2:3, :] * y) * g_ref[...] + b_ref[...]
    x1_out[...] = x1
    h2 = _layer_norm(x1) * (1.0 + mod_ref[0, 4:5, :]) + mod_ref[0, 3:4, :]
    h2_out[...] = _pack_pairs(h2)
    h2_hi, h2_lo = _split_bf16(h2)
    logits = _dot3(h2_hi, h2_lo, rwh_ref[...], rwl_ref[...]) + rb_ref[...]
    lane = lax.broadcasted_iota(jnp.int32, logits.shape, 1)
    lane_f = lane.astype(F32)
    cur = jnp.where(lane < n_experts, logits, -jnp.inf)
    vals, idxs = [], []
    for _ in range(TOP_K):
        m = cur.max(axis=-1, keepdims=True)
        ix = jnp.where(cur == m, lane_f, float(LANES)).min(axis=-1, keepdims=True)
        vals.append(m)
        idxs.append(ix)
        cur = jnp.where(lane_f == ix, -jnp.inf, cur)
    es = [jnp.exp(v - vals[0]) for v in vals]
    denom = es[0]
    for e in es[1:]:
        denom = denom + e
    idx_v = jnp.zeros(logits.shape, F32)
    wt_v = jnp.zeros(logits.shape, F32)
    for k in range(TOP_K):
        idx_v = jnp.where(lane == k, idxs[k], idx_v)
        wt_v = jnp.where(lane == k, es[k] / denom, wt_v)
    idx_out[...] = idx_v.astype(jnp.int32)
    wt_out[...] = wt_v


def _mix_out(xf, y_att, y_hy, y_na, mods_l, w_out_bf, ln_g, ln_b, router_wp, router_bp, tile_mod, n_tok,
             alpha, n_experts):
    d = xf.shape[1]
    tm = TOKEN_TILE
    row = lambda i, tmod: (i, 0)
    const = lambda i, tmod: (0, 0)
    grid_spec = pltpu.PrefetchScalarGridSpec(
        num_scalar_prefetch=1,
        grid=(n_tok // tm,),
        in_specs=[
            pl.BlockSpec((tm, d), row),
            pl.BlockSpec((tm, ATT_W), row),
            pl.BlockSpec((tm, HY_W), row),
            pl.BlockSpec((tm, NA_W), row),
            pl.BlockSpec((1, 6, d), lambda i, tmod: (tmod[i], 0, 0)),
            pl.BlockSpec((D_MIX, d), const),
            pl.BlockSpec((1, d), const),
            pl.BlockSpec((1, d), const),
            pl.BlockSpec((d, LANES), const),
            pl.BlockSpec((d, LANES), const),
            pl.BlockSpec((1, LANES), const),
        ],
        out_specs=[
            pl.BlockSpec((tm, d), row),
            pl.BlockSpec((tm, d // 2), row),
            pl.BlockSpec((tm, LANES), row),
            pl.BlockSpec((tm, LANES), row),
        ],
    )
    return pl.pallas_call(
        functools.partial(_mix_out_kernel, alpha=alpha, n_experts=n_experts),
        grid_spec=grid_spec,
        out_shape=[
            jax.ShapeDtypeStruct((n_tok, d), F32),
            jax.ShapeDtypeStruct((n_tok, d // 2), jnp.uint32),
            jax.ShapeDtypeStruct((n_tok, LANES), jnp.int32),
            jax.ShapeDtypeStruct((n_tok, LANES), F32),
        ],
        compiler_params=_cparams(("arbitrary",)),
        name="mix_out_norm_route",
    )(tile_mod, xf, y_att, y_hy, y_na, mods_l, w_out_bf, ln_g, ln_b, *_split_bf16(router_wp), router_bp)


def _dispatch_kernel(slot_ref, h_ref, xs_in, xs_hbm, sem, *, tm):
    del xs_in

    def row_copy(src_row, dst_row):
        return pltpu.make_async_copy(h_ref.at[pl.ds(src_row, 1)], xs_hbm.at[pl.ds(dst_row, 1)], sem)

    def issue(t, carry):
        for k in range(TOP_K):
            row_copy(t, slot_ref[t * TOP_K + k]).start(priority=k % 2)
        return carry

    lax.fori_loop(0, tm, issue, 0, unroll=8)

    def drain(t, carry):
        for k in range(TOP_K):
            row_copy(0, 0).wait()
        return carry

    lax.fori_loop(0, tm, drain, 0, unroll=8)


def _dispatch(h2p, slots_flat, n_slots):
    n_tok, hw = h2p.shape
    tm = TOKEN_TILE
    xs0 = jnp.zeros((n_slots, hw), h2p.dtype)
    return pl.pallas_call(
        functools.partial(_dispatch_kernel, tm=tm),
        grid=(n_tok // tm,),
        in_specs=[
            pl.BlockSpec((tm * TOP_K,), lambda i: (i,), memory_space=pltpu.SMEM),
            pl.BlockSpec((tm, hw), lambda i: (i, 0)),
            pl.BlockSpec(memory_space=pl.ANY),
        ],
        out_specs=pl.BlockSpec(memory_space=pl.ANY),
        out_shape=jax.ShapeDtypeStruct((n_slots, hw), h2p.dtype),
        scratch_shapes=[pltpu.SemaphoreType.DMA],
        input_output_aliases={2: 0},
        compiler_params=_cparams(("arbitrary",)),
        name="moe_dispatch",
    )(slots_flat, h2p, xs0)


def _expert_kernel(te_ref, nu_ref, x_ref, w1_ref, b1_ref, w2_ref, b2_ref, o_ref, w1_bf, w2_bf, *, d_expert):
    i = pl.program_id(0)

    @pl.when((i == 0) | (te_ref[i] != te_ref[jnp.maximum(i - 1, 0)]))
    def _():
        w1_bf[...] = w1_ref[0].astype(BF16)
        w2_bf[...] = w2_ref[0].astype(BF16)

    @pl.when(i < nu_ref[0])
    def _():
        x = _unpack_pairs(x_ref[...])
        hid = _dot(x, w1_bf[...]) + b1_ref[0]
        glu = jnp.minimum(hid[:, :d_expert], SWIGLU_LIMIT)
        lin = jnp.clip(hid[:, d_expert:], -SWIGLU_LIMIT, SWIGLU_LIMIT)
        act = glu * jax.nn.sigmoid(SWIGLU_ALPHA * glu) * (lin + 1.0)
        o_ref[...] = _dot(act.astype(BF16), w2_bf[...]) + b2_ref[0]

    @pl.when(i >= nu_ref[0])
    def _():
        o_ref[...] = jnp.zeros(o_ref.shape, o_ref.dtype)


def _experts(xs, tile_expert, n_used, w1, b1, w2, b2, layer):
    n_slots, hw = xs.shape
    n_exp = b1.shape[0]
    _, d, d2 = w1.shape
    d_expert = d2 // 2
    tm = MOE_TILE
    n_tiles = n_slots // tm
    e0 = layer * n_exp
    grid_spec = pltpu.PrefetchScalarGridSpec(
        num_scalar_prefetch=2,
        grid=(n_tiles,),
        in_specs=[
            pl.BlockSpec((tm, hw), lambda i, te, nu: (i, 0)),
            pl.BlockSpec((1, d, d2), lambda i, te, nu: (e0 + te[i], 0, 0)),
            pl.BlockSpec((1, 1, d2), lambda i, te, nu: (te[i], 0, 0)),
            pl.BlockSpec((1, d_expert, d), lambda i, te, nu: (e0 + te[i], 0, 0)),
            pl.BlockSpec((1, 1, d), lambda i, te, nu: (te[i], 0, 0)),
        ],
        out_specs=pl.BlockSpec((tm, d), lambda i, te, nu: (i, 0)),
        scratch_shapes=[pltpu.VMEM((d, d2), BF16), pltpu.VMEM((d_expert, d), BF16)],
    )
    return pl.pallas_call(
        functools.partial(_expert_kernel, d_expert=d_expert),
        grid_spec=grid_spec,
        out_shape=jax.ShapeDtypeStruct((n_slots, d), F32),
        compiler_params=_cparams(("arbitrary",)),
        name="moe_experts",
    )(tile_expert, n_used, xs, w1, b1.reshape(n_exp, 1, d2), w2, b2.reshape(n_exp, 1, d))


def _combine_kernel(tmod_ref, slot_ref, next_ref, x1_ref, wt_ref, mod_ref, g_ref, b_ref, ys_hbm, o_ref, buf, sem,
                    *, tm, alpha):
    del tmod_ref
    i = pl.program_id(0)

    def row_copy(src_row, b, k, t):
        return pltpu.make_async_copy(ys_hbm.at[pl.ds(src_row, 1)], buf.at[b, k, pl.ds(t, 1)], sem.at[b])

    def issue(ref, off, b):
        def body(t, carry):
            for k in range(TOP_K):
                row_copy(ref[off + t * TOP_K + k], b, k, t).start(priority=k % 2)
            return carry

        lax.fori_loop(0, tm, body, 0, unroll=8)

    def drain(b):
        def body(t, carry):
            for k in range(TOP_K):
                row_copy(0, b, k, t).wait()
            return carry

        lax.fori_loop(0, tm, body, 0, unroll=8)

    def finish(b):
        rows = pl.ds(b * tm, tm)
        wt = wt_ref[rows, :]
        moe = buf[b, 0] * wt[:, 0:1]
        for k in range(1, TOP_K):
            moe = moe + buf[b, k] * wt[:, k:k + 1]
        o_ref[rows, :] = _layer_norm(alpha * x1_ref[rows, :] + mod_ref[0, 5:6, :] * moe) * g_ref[...] + b_ref[...]

    @pl.when(i == 0)
    def _():
        issue(slot_ref, 0, 0)

    issue(slot_ref, tm * TOP_K, 1)
    drain(0)
    finish(0)

    @pl.when(i + 1 < pl.num_programs(0))
    def _():
        issue(next_ref, 0, 0)

    drain(1)
    finish(1)


def _combine(x1, wts, slots_flat, ys, mods_l, ln_g, ln_b, tile_mod, alpha):
    n_tok, d = x1.shape
    tm = TOKEN_TILE
    n_steps = n_tok // (2 * tm)
    assert n_tok % (2 * tm) == 0
    row = lambda i, tmod: (i, 0)
    const = lambda i, tmod: (0, 0)
    grid_spec = pltpu.PrefetchScalarGridSpec(
        num_scalar_prefetch=1,
        grid=(n_steps,),
        in_specs=[
            pl.BlockSpec((2 * tm * TOP_K,), lambda i, tmod: (i,), memory_space=pltpu.SMEM),
            pl.BlockSpec((tm * TOP_K,), lambda i, tmod: (jnp.minimum(2 * i + 2, 2 * n_steps - 1),),
                         memory_space=pltpu.SMEM),
            pl.BlockSpec((2 * tm, d), row),
            pl.BlockSpec((2 * tm, LANES), row),
            pl.BlockSpec((1, 6, d), lambda i, tmod: (tmod[2 * i], 0, 0)),
            pl.BlockSpec((1, d), const),
            pl.BlockSpec((1, d), const),
            pl.BlockSpec(memory_space=pl.ANY),
        ],
        out_specs=pl.BlockSpec((2 * tm, d), row),
        scratch_shapes=[pltpu.VMEM((2, TOP_K, tm, d), F32), pltpu.SemaphoreType.DMA((2,))],
    )
    return pl.pallas_call(
        functools.partial(_combine_kernel, tm=tm, alpha=alpha),
        grid_spec=grid_spec,
        out_shape=jax.ShapeDtypeStruct((n_tok, d), F32),
        compiler_params=_cparams(("arbitrary",)),
        name="moe_combine_norm",
    )(tile_mod, slots_flat, slots_flat, x1, wts, mods_l, ln_g, ln_b, ys)


def _routing_tables(idx4, n_experts, tile):
    n_tok = idx4.shape[0]
    blk = TOKEN_TILE
    hit = idx4[:, :, None] == jnp.arange(n_experts, dtype=jnp.int32)[None, None, :]
    onehot = hit.any(axis=1)
    blocks = onehot.reshape(n_tok // blk, blk, n_experts).astype(BF16)
    tri = jnp.asarray(np.tril(np.ones((blk, blk), np.float32), -1), BF16)
    within = jnp.einsum("ij,bjk->bik", tri, blocks, preferred_element_type=F32).astype(jnp.int32)
    block_tot = onehot.reshape(n_tok // blk, blk, n_experts).sum(axis=1, dtype=jnp.int32)
    block_pre = jnp.cumsum(block_tot, axis=0) - block_tot
    pos = (within + block_pre[:, None, :]).reshape(n_tok, n_experts)
    counts = block_tot.sum(axis=0)
    padded = ((counts + tile - 1) // tile) * tile
    ends = jnp.cumsum(padded)
    offs = ends - padded
    slots = jnp.sum(jnp.where(hit, (offs[None, :] + pos)[:, None, :], 0), axis=2)
    n_tiles = (n_tok * TOP_K) // tile + n_experts
    tile_start = jnp.arange(n_tiles, dtype=jnp.int32) * tile
    tile_expert = jnp.minimum(jnp.sum(tile_start[:, None] >= ends[None, :], axis=1), n_experts - 1).astype(jnp.int32)
    n_used = (ends[-1] // tile).astype(jnp.int32)
    last = tile_expert[jnp.maximum(n_used - 1, 0)]
    tile_expert = jnp.where(jnp.arange(n_tiles) < n_used, tile_expert, last)
    return slots.astype(jnp.int32).reshape(-1), tile_expert, n_used.reshape(1), n_tiles * tile


def _rope_tables(n, grid_w, extra_rows):
    t = np.arange(n)
    row = (t // grid_w).astype(np.float32)[:, None]
    col = (t % grid_w).astype(np.float32)[:, None]
    axis_dim = HEAD_DIM // 2
    inv_freq = (ROPE_THETA ** (-np.arange(0, axis_dim, 2, dtype=np.float32) / axis_dim)).astype(np.float32)
    ang_r = row * inv_freq
    ang_c = col * inv_freq
    ang = np.concatenate([ang_r, ang_r, ang_c, ang_c], axis=-1)
    cos = np.cos(ang).astype(np.float32)
    sin = np.sin(ang).astype(np.float32)
    sign = np.where((np.arange(HEAD_DIM) % 32) < 16, -1.0, 1.0).astype(np.float32)
    sin = sin * sign[None, :]
    cos = np.concatenate([cos, np.ones((extra_rows, HEAD_DIM), np.float32)], axis=0)
    sin = np.concatenate([sin, np.zeros((extra_rows, HEAD_DIM), np.float32)], axis=0)
    reps = LANES // HEAD_DIM
    return jnp.asarray(np.tile(cos, (1, reps))), jnp.asarray(np.tile(sin, (1, reps)))


def _forward(x, c, ctx, c_ctx, ada_w, ada_b, w_in, w_out, q_gain, k_gain, hy_conv_w, hy_conv_b,
             hy_w1, hy_b1, hy_freq, hy_w2, hy_b2, hy_w3, hy_b3, hy_d, na_rpb, ln1_g, ln1_b,
             ln2_g, ln2_b, router_w, router_b, exp_w1, exp_b1, exp_w2, exp_b2, *, grid_w):
    nb, n, d = x.shape
    m = ctx.shape[1]
    depth = ada_w.shape[0]
    n_experts = router_w.shape[-1]
    alpha = (2.0 * depth) ** 0.25
    tm = TOKEN_TILE
    assert n % tm == 0 and m % tm == 0 and n % m == 0 and n_experts <= LANES
    assert (n // tm) % 2 == 0 and (nb * m // tm) % 2 == 0
    t_lat, t_all = nb * n, nb * (n + m)

    xf = jnp.concatenate([x.reshape(nb * n, d), ctx.reshape(nb * m, d)], axis=0)
    cvec = jnp.zeros((SUBLANES, d), F32).at[:nb].set(c).at[nb].set(c_ctx)
    mods = _adaln(cvec, ada_w, ada_b).reshape(depth, SUBLANES, 6, d)

    tiles = np.arange(t_all // tm)
    lat_tile = tiles < t_lat // tm
    tile_mod = jnp.asarray(np.where(lat_tile, tiles // (n // tm), nb), jnp.int32)
    tile_pos = jnp.asarray(np.where(lat_tile, tiles % (n // tm), n // tm), jnp.int32)
    cos_t, sin_t = _rope_tables(n, grid_w, tm)
    gmat = jnp.asarray(np.kron(np.eye(ATT_Q_HEADS), np.ones((HEAD_DIM, HEAD_DIM))), BF16)
    dft_lat = _dft_constants(n)
    dft_ctx = _dft_constants(m)

    for l in range(depth):
        need_ctx = l < depth - 1
        n_tok = t_all if need_ctx else t_lat
        q, k, v, hyp, nq, nk, nv = _proj(
            xf, mods[l], w_in[l].astype(BF16), jnp.tile(q_gain[l], ATT_Q_HEADS)[None, :],
            jnp.tile(k_gain[l], ATT_KV_HEADS)[None, :], gmat, cos_t, sin_t, tile_mod, tile_pos)
        y_att = _gqa(q, k, v, nb, n, m, need_ctx)
        filt = (hy_w1[l], hy_b1[l], hy_freq[l], hy_w2[l], hy_b2[l], hy_w3[l], hy_b3[l])
        gre, gim = _hyena_filter_spectrum(n, dft_lat, *filt)
        cb = hy_conv_b[l][None, :]
        dsk = hy_d[l][None, :]
        y_hy = _hyena(hyp, hy_conv_w[l], cb, dsk, gre, gim, dft_lat, nb, n, 0)
        if need_ctx:
            gre_c, gim_c = _hyena_filter_spectrum(m, dft_ctx, *filt)
            y_hy_c = _hyena(hyp, hy_conv_w[l], cb, dsk, gre_c, gim_c, dft_ctx, nb, m, t_lat // m)
            y_hy = jnp.concatenate([y_hy, y_hy_c], axis=0)
        bias_tab = _na_bias_table(na_rpb[l], grid_w, n // grid_w)
        y_na = _na(nq, nk, nv, bias_tab, nb, n, m, grid_w, need_ctx)

        router_wp = jnp.zeros((d, LANES), F32).at[:, :n_experts].set(router_w[l])
        router_bp = jnp.zeros((1, LANES), F32).at[0, :n_experts].set(router_b[l])
        x1, h2p, idx, wts = _mix_out(xf, y_att, y_hy, y_na, mods[l], w_out[l].astype(BF16), ln1_g[l][None, :],
                                     ln1_b[l][None, :], router_wp, router_bp, tile_mod, n_tok, alpha, n_experts)
        slots, tile_expert, n_used, n_slots = _routing_tables(idx[:, :TOP_K], n_experts, MOE_TILE)
        xs = _dispatch(h2p, slots, n_slots)
        ys = _experts(xs, tile_expert, n_used, exp_w1.reshape((-1,) + exp_w1.shape[2:]), exp_b1[l],
                      exp_w2.reshape((-1,) + exp_w2.shape[2:]), exp_b2[l], l)
        xf = _combine(x1, wts, slots, ys, mods[l], ln2_g[l][None, :], ln2_b[l][None, :], tile_mod, alpha)
    return xf[:t_lat].reshape(nb, n, d)


def kernel(x, c, ctx, c_ctx, ada_w, ada_b, w_in, w_out, q_gain, k_gain, hy_conv_w, hy_conv_b, hy_w1, hy_b1,
           hy_freq, hy_w2, hy_b2, hy_w3, hy_b3, hy_d, na_rpb, ln1_g, ln1_b, ln2_g, ln2_b, router_w, router_b,
           exp_w1, exp_b1, exp_w2, exp_b2):
    return _forward(x, c, ctx, c_ctx, ada_w, ada_b, w_in, w_out, q_gain, k_gain, hy_conv_w, hy_conv_b, hy_w1,
                    hy_b1, hy_freq, hy_w2, hy_b2, hy_w3, hy_b3, hy_d, na_rpb, ln1_g, ln1_b, ln2_g, ln2_b,
                    router_w, router_b, exp_w1, exp_b1, exp_w2, exp_b2, grid_w=64)
```

```python
import cmath
import functools
import math

import jax
import jax.numpy as jnp
import numpy as np
from jax import lax
from jax.experimental import pallas as pl
from jax.experimental.pallas import tpu as pltpu

F32 = jnp.float32
BF16 = jnp.bfloat16
HIGHEST = lax.Precision.HIGHEST

HEAD_DIM = 64
ATT_Q_HEADS = 8
ATT_KV_HEADS = 2
ATT_GROUP = ATT_Q_HEADS // ATT_KV_HEADS
ATT_W = ATT_Q_HEADS * HEAD_DIM
KV_W = ATT_KV_HEADS * HEAD_DIM
HY_W = 256
HY_SHORT = 3
HY_EMB = 33
HY_BANDS = (HY_EMB - 1) // 2
HY_FO = 64
HY_MIN_DECAY = math.log(1e-2) / 1.5
HY_MAX_DECAY = math.log(1e-2) / 0.3
NA_HEADS = 4
NA_W = NA_HEADS * HEAD_DIM
NA_WIN_ROWS = 8
NA_WIN_COLS = 16
D_MIX = ATT_W + HY_W + NA_W
D_IN = ATT_W + 2 * KV_W + 3 * HY_W + 3 * NA_W
TOP_K = 4
SWIGLU_ALPHA = 1.702
SWIGLU_LIMIT = 7.0
ROPE_THETA = 10000.0
NORM_EPS = 1e-6
Q_SCALE = HEAD_DIM ** -0.5

LANES = 128
SUBLANES = 8
NEG_BIG = -1e30

TOKEN_TILE = 256
ATT_Q_TILE = 256
NA_Q_ROWS = 8
NA_K_ROWS = 16
FFT_N1 = 8
HY_CB = 128
FREQ_CHUNK = 256
MOE_TILE = 512
EXPERT_HIDDEN_CHUNK = 256
VMEM_LIMIT = 56 * 1024 * 1024


def _cparams(sem, vmem=VMEM_LIMIT):
    return pltpu.CompilerParams(dimension_semantics=sem, vmem_limit_bytes=vmem)


def _layer_norm(x):
    mu = jnp.mean(x, axis=-1, keepdims=True)
    xc = x - mu
    var = jnp.mean(xc * xc, axis=-1, keepdims=True)
    return xc * lax.rsqrt(var + NORM_EPS)


def _dot(a, b):
    return jnp.dot(a, b, preferred_element_type=F32)


def _dot_nt(a, b):
    return lax.dot_general(a, b, (((1,), (1,)), ((), ())), preferred_element_type=F32)


def _split_bf16(x):
    hi = x.astype(BF16)
    lo = (x - hi.astype(F32)).astype(BF16)
    return hi, lo


def _dot3(a_hi, a_lo, b_hi, b_lo):
    return _dot(a_hi, b_hi) + _dot(a_hi, b_lo) + _dot(a_lo, b_hi)


def _adaln_kernel(c_ref, w_ref, b_ref, o_ref):
    c = c_ref[...]
    a = c * jax.nn.sigmoid(c)
    o_ref[0] = jnp.dot(a, w_ref[0], preferred_element_type=F32, precision=HIGHEST) + b_ref[0]


def _adaln(cvec, ada_w, ada_b):
    n_layers, d, d6 = ada_w.shape
    bn = d6 // 4 if (d6 // 4) % LANES == 0 else d6
    return pl.pallas_call(
        _adaln_kernel,
        grid=(n_layers, d6 // bn),
        in_specs=[
            pl.BlockSpec((SUBLANES, d), lambda l, j: (0, 0)),
            pl.BlockSpec((1, d, bn), lambda l, j: (l, 0, j)),
            pl.BlockSpec((1, 1, bn), lambda l, j: (l, 0, j)),
        ],
        out_specs=pl.BlockSpec((1, SUBLANES, bn), lambda l, j: (l, 0, j)),
        out_shape=jax.ShapeDtypeStruct((n_layers, SUBLANES, d6), F32),
        compiler_params=_cparams(("arbitrary", "arbitrary")),
        name="adaln",
    )(cvec, ada_w, ada_b.reshape(n_layers, 1, d6))


def _head_norm(a, g_ref, gain):
    w = a.shape[1]
    sq = a * a
    hi, lo = _split_bf16(sq)
    g = g_ref[0:w, 0:w]
    ss = _dot(hi, g) + _dot(lo, g)
    return a * lax.rsqrt(ss * (1.0 / HEAD_DIM) + NORM_EPS) * gain


def _rope(a, cos, sin_signed):
    w = a.shape[1]
    reps = w // LANES
    c = jnp.concatenate([cos] * reps, axis=1) if reps > 1 else cos
    s = jnp.concatenate([sin_signed] * reps, axis=1) if reps > 1 else sin_signed
    lane = lax.broadcasted_iota(jnp.int32, a.shape, 1)
    first = (lane & 31) < 16
    rot = jnp.where(first, pltpu.roll(a, w - 16, axis=1), pltpu.roll(a, 16, axis=1))
    return a * c + rot * s


def _proj_kernel(tmod_ref, tpos_ref, x_ref, mod_ref, w_ref, qg_ref, kg_ref, g_ref, cos_ref, sin_ref,
                 q_out, k_out, v_out, hy_out, nq_out, nk_out, nv_out):
    del tmod_ref, tpos_ref
    h = _layer_norm(x_ref[...]) * (1.0 + mod_ref[0, 1:2, :]) + mod_ref[0, 0:1, :]
    p = _dot(h.astype(BF16), w_ref[...])
    o = 0
    aq = p[:, o:o + ATT_W]; o += ATT_W
    ak = p[:, o:o + KV_W]; o += KV_W
    av = p[:, o:o + KV_W]; o += KV_W
    hy = p[:, o:o + 3 * HY_W]; o += 3 * HY_W
    nq = p[:, o:o + NA_W]; o += NA_W
    nk = p[:, o:o + NA_W]; o += NA_W
    nv = p[:, o:o + NA_W]
    cos = cos_ref[...]
    sin = sin_ref[...]
    q = _rope(_head_norm(aq, g_ref, qg_ref[...]), cos, sin) * Q_SCALE
    k = _rope(_head_norm(ak, g_ref, kg_ref[...]), cos, sin)
    q_out[...] = q.astype(BF16)
    k_out[...] = k.astype(BF16)
    v_out[...] = av.astype(BF16)
    hy_out[...] = hy
    nq_out[...] = (nq * Q_SCALE).astype(BF16)
    nk_out[...] = nk.astype(BF16)
    nv_out[...] = nv.astype(BF16)


def _proj(xf, mods_l, w_in_bf, q_gain, k_gain, gmat, cos_t, sin_t, tile_mod, tile_pos):
    t, d = xf.shape
    tm = TOKEN_TILE
    n_tiles = t // tm
    row = lambda i, tmod, tpos: (i, 0)
    const = lambda i, tmod, tpos: (0, 0)
    widths = (ATT_W, KV_W, KV_W, 3 * HY_W, NA_W, NA_W, NA_W)
    dtypes = (BF16, BF16, BF16, F32, BF16, BF16, BF16)
    grid_spec = pltpu.PrefetchScalarGridSpec(
        num_scalar_prefetch=2,
        grid=(n_tiles,),
        in_specs=[
            pl.BlockSpec((tm, d), row),
            pl.BlockSpec((1, 6, d), lambda i, tmod, tpos: (tmod[i], 0, 0)),
            pl.BlockSpec((d, D_IN), const),
            pl.BlockSpec((1, ATT_W), const),
            pl.BlockSpec((1, KV_W), const),
            pl.BlockSpec((ATT_W, ATT_W), const),
            pl.BlockSpec((tm, LANES), lambda i, tmod, tpos: (tpos[i], 0)),
            pl.BlockSpec((tm, LANES), lambda i, tmod, tpos: (tpos[i], 0)),
        ],
        out_specs=[pl.BlockSpec((tm, w), row) for w in widths],
    )
    return pl.pallas_call(
        _proj_kernel,
        grid_spec=grid_spec,
        out_shape=[jax.ShapeDtypeStruct((t, w), dt) for w, dt in zip(widths, dtypes)],
        compiler_params=_cparams(("arbitrary",)),
        name="ln_mod_proj",
    )(tile_mod, tile_pos, xf, mods_l, w_in_bf, q_gain, k_gain, gmat, cos_t, sin_t)


def _attend(q, segs, n_kv, grp, bias=None):
    tq = q.shape[0]
    hd = HEAD_DIM
    outs = []
    for j in range(n_kv):
        heads = [q[:, (j * grp + g) * hd:(j * grp + g + 1) * hd] for g in range(grp)]
        qs = jnp.concatenate(heads, axis=0) if grp > 1 else heads[0]
        scores = []
        for si, (k, _) in enumerate(segs):
            s = _dot_nt(qs, k[:, j * hd:(j + 1) * hd])
            if bias is not None and bias[j][si] is not None:
                s = s + bias[j][si]
            scores.append(s)
        m = scores[0].max(axis=-1, keepdims=True)
        for s in scores[1:]:
            m = jnp.maximum(m, s.max(axis=-1, keepdims=True))
        acc = None
        for s, (_, v) in zip(scores, segs):
            p = jnp.exp(s - m).astype(BF16)
            vj = v[:, j * hd:(j + 1) * hd]
            ones = (lax.broadcasted_iota(jnp.int32, vj.shape, 1) == 0).astype(BF16)
            pv = _dot(p, jnp.concatenate([vj, ones], axis=1))
            acc = pv if acc is None else acc + pv
        o = acc[:, :hd] / acc[:, hd:hd + 1]
        for g in range(grp):
            outs.append(o[g * tq:(g + 1) * tq])
    return jnp.concatenate(outs, axis=1)


def _gqa_kernel(q_ref, kl_ref, vl_ref, kc_ref, vc_ref, o_ref):
    segs = [(kl_ref[...], vl_ref[...]), (kc_ref[...], vc_ref[...])]
    o_ref[...] = _attend(q_ref[...], segs, ATT_KV_HEADS, ATT_GROUP).astype(o_ref.dtype)


def _gqa(q, k, v, nb, n, m, with_ctx):
    t = q.shape[0]
    tq = ATT_Q_TILE
    n_lat = n // tq
    ctx_blk0 = nb * n // m
    qmap = lambda b, i: (b * n_lat + i, 0)
    lat = lambda b, i: (b, 0)
    ctx = lambda b, i: (ctx_blk0 + b, 0)
    y = pl.pallas_call(
        _gqa_kernel,
        grid=(nb, n_lat),
        in_specs=[
            pl.BlockSpec((tq, ATT_W), qmap),
            pl.BlockSpec((n, KV_W), lat),
            pl.BlockSpec((n, KV_W), lat),
            pl.BlockSpec((m, KV_W), ctx),
            pl.BlockSpec((m, KV_W), ctx),
        ],
        out_specs=pl.BlockSpec((tq, ATT_W), qmap),
        out_shape=jax.ShapeDtypeStruct((nb * n, ATT_W), BF16),
        compiler_params=_cparams(("arbitrary", "arbitrary")),
        name="gqa_attention",
    )(q, k, v, k, v)
    if not with_ctx:
        return y
    yc = _ctx_attention(q, k, v, nb, n, m, ATT_KV_HEADS, ATT_GROUP, ATT_W, "gqa_ctx_attention")
    return jnp.concatenate([y, yc], axis=0)


def _na_kernel(q_ref, kl_ref, vl_ref, kc_ref, vc_ref, bias_ref, o_ref, *, grid_w, grid_rows):
    i = pl.program_id(1)
    krow0 = jnp.clip(i * NA_Q_ROWS - NA_WIN_ROWS // 2, 0, grid_rows - NA_K_ROWS)
    start = pl.multiple_of(krow0 * grid_w, 4 * grid_w)
    nk = NA_K_ROWS * grid_w
    segs = [(kl_ref[pl.ds(start, nk), :], vl_ref[pl.ds(start, nk), :]), (kc_ref[...], vc_ref[...])]
    bias = [[bias_ref[0, h], None] for h in range(NA_HEADS)]
    o_ref[...] = _attend(q_ref[...], segs, NA_HEADS, 1, bias).astype(o_ref.dtype)


def _na_bias_table(rpb, grid_w, grid_rows):
    qr, kr = NA_Q_ROWS, NA_K_ROWS
    n_rr, n_rc = 2 * NA_WIN_ROWS - 1, 2 * NA_WIN_COLS - 1
    c = np.arange(grid_w)[:, None]
    kcol = np.arange(grid_w)[None, :]
    cs = np.clip(c - NA_WIN_COLS // 2, 0, grid_w - NA_WIN_COLS)
    col_ok = (kcol >= cs) & (kcol < cs + NA_WIN_COLS)
    csel = ((kcol - c + NA_WIN_COLS - 1)[:, :, None] == np.arange(n_rc)) & col_ok[:, :, None]
    rsel, mask = [], []
    for r0 in (0, qr, grid_rows - qr):
        k0 = int(np.clip(r0 - NA_WIN_ROWS // 2, 0, grid_rows - kr))
        r = r0 + np.arange(qr)[:, None]
        krow = k0 + np.arange(kr)[None, :]
        rs = np.clip(r - NA_WIN_ROWS // 2, 0, grid_rows - NA_WIN_ROWS)
        row_ok = (krow >= rs) & (krow < rs + NA_WIN_ROWS)
        rsel.append(((krow - r + NA_WIN_ROWS - 1)[:, :, None] == np.arange(n_rr)) & row_ok[:, :, None])
        ok = row_ok[:, None, :, None] & col_ok[None, :, None, :]
        mask.append(np.where(ok, 0.0, NEG_BIG))
    rsel = jnp.asarray(np.stack(rsel), F32)
    mask = jnp.asarray(np.stack(mask), F32)
    vals = jnp.einsum("tjia,hab,ckb->thjcik", rsel, rpb, jnp.asarray(csel, F32), precision=HIGHEST)
    return (vals + mask[:, None]).reshape(3, NA_HEADS, qr * grid_w, kr * grid_w)


def _na(nq, nk, nv, bias_tab, nb, n, m, grid_w, with_ctx):
    t = nq.shape[0]
    grid_rows = n // grid_w
    tq = NA_Q_ROWS * grid_w
    n_lat = n // tq
    ctx_blk0 = nb * n // m
    lat = lambda b, i: (b, 0)
    ctx = lambda b, i: (ctx_blk0 + b, 0)
    qmap = lambda b, i: (b * n_lat + i, 0)

    def bmap(b, i):
        return (jnp.where(i == 0, 0, jnp.where(i >= n_lat - 1, 2, 1)), 0, 0, 0)

    y = pl.pallas_call(
        functools.partial(_na_kernel, grid_w=grid_w, grid_rows=grid_rows),
        grid=(nb, n_lat),
        in_specs=[
            pl.BlockSpec((tq, NA_W), qmap),
            pl.BlockSpec((n, NA_W), lat),
            pl.BlockSpec((n, NA_W), lat),
            pl.BlockSpec((m, NA_W), ctx),
            pl.BlockSpec((m, NA_W), ctx),
            pl.BlockSpec((1, NA_HEADS, tq, NA_K_ROWS * grid_w), bmap),
        ],
        out_specs=pl.BlockSpec((tq, NA_W), qmap),
        out_shape=jax.ShapeDtypeStruct((nb * n, NA_W), BF16),
        compiler_params=_cparams(("arbitrary", "arbitrary")),
        name="neighbourhood_attention",
    )(nq, nk, nv, nk, nv, bias_tab)
    if not with_ctx:
        return y
    yc = _ctx_attention(nq, nk, nv, nb, n, m, NA_HEADS, 1, NA_W, "na_ctx_attention")
    return jnp.concatenate([y, yc], axis=0)


def _ctx_attn_kernel(q_ref, k_ref, v_ref, o_ref, *, n_kv, grp):
    segs = [(k_ref[...], v_ref[...])]
    o_ref[...] = _attend(q_ref[...], segs, n_kv, grp).astype(o_ref.dtype)


def _ctx_attention(q, k, v, nb, n, m, n_kv, grp, width, name):
    ctx_blk0 = nb * n // m
    ctx = lambda b: (ctx_blk0 + b, 0)
    kw = n_kv * HEAD_DIM
    return pl.pallas_call(
        functools.partial(_ctx_attn_kernel, n_kv=n_kv, grp=grp),
        grid=(nb,),
        in_specs=[
            pl.BlockSpec((m, width), ctx),
            pl.BlockSpec((m, kw), ctx),
            pl.BlockSpec((m, kw), ctx),
        ],
        out_specs=pl.BlockSpec((m, width), lambda b: (b, 0)),
        out_shape=jax.ShapeDtypeStruct((nb * m, width), BF16),
        compiler_params=_cparams(("arbitrary",)),
        name=name,
    )(q, k, v)


def _dft_constants(seq_len):
    n = 2 * seq_len
    n2 = n // FFT_N1
    idx = np.arange(n2, dtype=np.int64)
    ang = 2.0 * np.pi * ((idx[:, None] * idx[None, :]) % n2).astype(np.float64) / n2
    cmat, smat = np.cos(ang), np.sin(ang)

    def split(a):
        hi = jnp.asarray(a, F32).astype(BF16)
        lo = (jnp.asarray(a, F32) - hi.astype(F32)).astype(BF16)
        return hi, lo

    def col_chunks(a):
        return np.transpose(a.reshape(a.shape[0], n2 // chunk, chunk), (1, 0, 2))

    chunk = min(FREQ_CHUNK, n2)
    tw_ang = 2.0 * np.pi * (idx[:, None] * np.arange(FFT_N1)[None, :]).astype(np.float64) / n
    return dict(
        n2=n2, chunk=chunk,
        fwd_c=split(cmat), fwd_s=split(smat),
        inv_c=split(col_chunks(cmat[: n2 // 2] / n)),
        inv_s=split(col_chunks(smat[: n2 // 2] / n)),
        tw_re=jnp.asarray(np.cos(tw_ang), F32), tw_im=jnp.asarray(-np.sin(tw_ang), F32),
    )


def _cmul_const(a, w):
    re, im = a
    if abs(w - 1) < 1e-12:
        return a
    if abs(w + 1) < 1e-12:
        return (-re, -im)
    if abs(w + 1j) < 1e-12:
        return (im, -re)
    if abs(w - 1j) < 1e-12:
        return (-im, re)
    return (re * w.real - im * w.imag, re * w.imag + im * w.real)


def _fft_blocks(xs, sign):
    n = len(xs)
    if n == 1:
        return xs
    even = _fft_blocks(xs[0::2], sign)
    odd = _fft_blocks(xs[1::2], sign)
    out = [None] * n
    for k in range(n // 2):
        t = _cmul_const(odd[k], cmath.exp(sign * 2j * cmath.pi * k / n))
        out[k] = (even[k][0] + t[0], even[k][1] + t[1])
        out[k + n // 2] = (even[k][0] - t[0], even[k][1] - t[1])
    return out


def _lane_blocks(a):
    return [a[:, j * LANES:(j + 1) * LANES] for j in range(a.shape[1] // LANES)]


def _forward_spectrum(xh, xl, rows, ch_ref, cl_ref, sh_ref, sl_ref, twr_ref, twi_ref):
    a_re = _dot3(ch_ref[rows, :], cl_ref[rows, :], xh, xl)
    a_im = -_dot3(sh_ref[rows, :], sl_ref[rows, :], xh, xl)
    blocks = []
    for s1, (re, im) in enumerate(zip(_lane_blocks(a_re), _lane_blocks(a_im))):
        tr = twr_ref[rows, s1:s1 + 1]
        ti = twi_ref[rows, s1:s1 + 1]
        blocks.append((re * tr - im * ti, re * ti + im * tr))
    return _fft_blocks(blocks, -1)


def _filter_kernel(emb_ref, w1_ref, b1_ref, fr_ref, w2_ref, b2_ref, w3f_ref, w3b_ref, b3f_ref, b3b_ref,
                   dec_ref, ch_ref, cl_ref, sh_ref, sl_ref, twr_ref, twi_ref, gre_ref, gim_ref, g_scr, hid_scr, *,
                   seq_len, chunk):
    n = 2 * seq_len
    n2 = n // FFT_N1
    def mm(a, w_ref):
        return _dot3(*_split_bf16(a), *_split_bf16(w_ref[...]))

    @pl.when(pl.program_id(0) == 0)
    def _():
        hid1 = jnp.sin(fr_ref[0:1, :] * (mm(emb_ref[...], w1_ref) + b1_ref[...]))
        hid_scr[...] = jnp.sin(fr_ref[1:2, :] * (mm(hid1, w2_ref) + b2_ref[...]))

    hid = hid_scr[...]
    f_fwd = mm(hid, w3f_ref) + b3f_ref[...]
    f_bwd = mm(hid, w3b_ref) + b3b_ref[...]
    row = lax.broadcasted_iota(jnp.int32, f_fwd.shape, 0)
    t_pos = emb_ref[:, 0:1]
    g = jnp.where(row < seq_len, f_fwd, jnp.where(row > seq_len, f_bwd, 0.0))
    g = g * jnp.exp(-t_pos * jnp.abs(dec_ref[...]))
    g = g * lax.rsqrt(jnp.sum(g * g, axis=0, keepdims=True) + NORM_EPS)
    g_scr[...] = g
    x = jnp.concatenate([g_scr[pl.ds(s1, n2, stride=FFT_N1), :] for s1 in range(FFT_N1)], axis=1)
    xh, xl = _split_bf16(x)

    def freq_chunk(ci, carry):
        rows = pl.ds(pl.multiple_of(ci * chunk, chunk), chunk)
        spec = _forward_spectrum(xh, xl, rows, ch_ref, cl_ref, sh_ref, sl_ref, twr_ref, twi_ref)
        gre_ref[0, rows, :] = jnp.concatenate([b[0] for b in spec], axis=1)
        gim_ref[0, rows, :] = jnp.concatenate([b[1] for b in spec], axis=1)
        return carry

    lax.fori_loop(0, n2 // chunk, freq_chunk, 0)


def _filter_embedding(seq_len):
    n = 2 * seq_len
    pos = np.concatenate([np.arange(seq_len), [0], np.arange(seq_len - 1, 0, -1)])
    t = np.linspace(0.0, 1.0, seq_len, dtype=np.float32)[pos][:, None]
    w = ((2.0 * math.pi / seq_len) * np.arange(seq_len, dtype=np.float32))[pos][:, None]
    bands = np.linspace(1e-4, HY_BANDS - 1, HY_BANDS, dtype=np.float32)[None, :]
    z = np.concatenate([t, np.cos(bands * w), -np.sin(bands * w)], axis=-1).astype(np.float32)
    emb = np.zeros((n, LANES), np.float32)
    emb[:, :HY_EMB] = z
    return jnp.asarray(emb)


def _hyena_filter_spectrum(seq_len, consts, f_w1, f_b1, f_freq, f_w2, f_b2, f_w3, f_b3):
    n = 2 * seq_len
    n2 = consts["n2"]
    ncb = HY_W // HY_CB
    emb = _filter_embedding(seq_len)
    w1p = jnp.zeros((LANES, HY_FO), F32).at[:HY_EMB].set(f_w1)
    deltas = jnp.asarray(np.linspace(HY_MIN_DECAY, HY_MAX_DECAY, HY_W, dtype=np.float32))[None, :]
    const = lambda j: (0, 0)
    out_sds = jax.ShapeDtypeStruct((ncb, n2, FFT_N1 * HY_CB), F32)
    return pl.pallas_call(
        functools.partial(_filter_kernel, seq_len=seq_len, chunk=consts["chunk"]),
        grid=(ncb,),
        in_specs=[
            pl.BlockSpec((n, LANES), const),
            pl.BlockSpec((LANES, HY_FO), const),
            pl.BlockSpec((1, HY_FO), const),
            pl.BlockSpec((2, HY_FO), const),
            pl.BlockSpec((HY_FO, HY_FO), const),
            pl.BlockSpec((1, HY_FO), const),
            pl.BlockSpec((HY_FO, HY_CB), lambda j: (0, j)),
            pl.BlockSpec((HY_FO, HY_CB), lambda j: (0, ncb + j)),
            pl.BlockSpec((1, HY_CB), lambda j: (0, j)),
            pl.BlockSpec((1, HY_CB), lambda j: (0, ncb + j)),
            pl.BlockSpec((1, HY_CB), lambda j: (0, j)),
        ] + [pl.BlockSpec((n2, n2), const, pipeline_mode=pl.Buffered(1))] * 4
          + [pl.BlockSpec((n2, FFT_N1), const, pipeline_mode=pl.Buffered(1))] * 2,
        out_specs=[pl.BlockSpec((1, n2, FFT_N1 * HY_CB), lambda j: (j, 0, 0))] * 2,
        out_shape=[out_sds, out_sds],
        scratch_shapes=[pltpu.VMEM((n, HY_CB), F32), pltpu.VMEM((n, HY_FO), F32)],
        compiler_params=_cparams(("arbitrary",)),
        name=f"hyena_filter_{seq_len}",
    )(emb, w1p, f_b1[None, :], f_freq, f_w2, f_b2[None, :], f_w3, f_w3, f_b3[None, :], f_b3[None, :], deltas,
      *consts["fwd_c"], *consts["fwd_s"], consts["tw_re"], consts["tw_im"])


def _short_conv(p, w_ref, b_ref):
    n = p.shape[0]
    row = lax.broadcasted_iota(jnp.int32, p.shape, 0)
    prev = jnp.where(row == 0, 0.0, pltpu.roll(p, 1, axis=0))
    nxt = jnp.where(row == n - 1, 0.0, pltpu.roll(p, n - 1, axis=0))
    return prev * w_ref[0:1, :] + p * w_ref[1:2, :] + nxt * w_ref[2:3, :] + b_ref[...]


def _hyena_kernel(p0_ref, p1_ref, p2_ref, w0_ref, w1_ref, w2_ref, b0_ref, b1_ref, b2_ref, d_ref,
                  gre_ref, gim_ref, fch_ref, fcl_ref, fsh_ref, fsl_ref, ich_ref, icl_ref, ish_ref, isl_ref,
                  twr_ref, twi_ref, o_ref, z_scr, y_scr, acc_scr, *, seq_len, chunk):
    n2 = 2 * seq_len // FFT_N1
    half = n2 // 2
    x1 = _short_conv(p1_ref[...], w1_ref, b1_ref)
    v = _short_conv(p2_ref[...], w2_ref, b2_ref)
    z_scr[...] = x1 * v
    x = jnp.concatenate([z_scr[pl.ds(s1, half, stride=FFT_N1), :] for s1 in range(FFT_N1)], axis=1)
    xh, xl = _split_bf16(x)
    acc_scr[...] = jnp.zeros(acc_scr.shape, F32)

    def freq_chunk(ci, carry):
        rows = pl.ds(pl.multiple_of(ci * chunk, chunk), chunk)
        zf = _forward_spectrum(xh, xl, rows, fch_ref, fcl_ref, fsh_ref, fsl_ref, twr_ref, twi_ref)
        gre = _lane_blocks(gre_ref[0, rows, :])
        gim = _lane_blocks(gim_ref[0, rows, :])
        prod = [(zr * gr - zi * gi, zr * gi + zi * gr) for (zr, zi), gr, gi in zip(zf, gre, gim)]
        back = _fft_blocks(prod, +1)
        c_re, c_im = [], []
        for s1, (re, im) in enumerate(back):
            tr = twr_ref[rows, s1:s1 + 1]
            ti = -twi_ref[rows, s1:s1 + 1]
            c_re.append(re * tr - im * ti)
            c_im.append(re * ti + im * tr)
        rh, rl = _split_bf16(jnp.concatenate(c_re, axis=1))
        ih, il = _split_bf16(jnp.concatenate(c_im, axis=1))
        acc_scr[...] += _dot3(ich_ref[ci], icl_ref[ci], rh, rl) - _dot3(ish_ref[ci], isl_ref[ci], ih, il)
        return carry

    lax.fori_loop(0, n2 // chunk, freq_chunk, 0)
    for s1 in range(FFT_N1):
        y_scr[pl.ds(s1, half, stride=FFT_N1), :] = acc_scr[:, s1 * LANES:(s1 + 1) * LANES]
    x0 = _short_conv(p0_ref[...], w0_ref, b0_ref)
    o_ref[...] = (x0 * (y_scr[...] + z_scr[...] * d_ref[...])).astype(o_ref.dtype)


def _hyena(hyp, conv_w, conv_b, d_skip, gre, gim, consts, nb, seq_len, row_blk0):
    n2 = consts["n2"]
    chunk = consts["chunk"]
    ncb = HY_W // HY_CB
    half = n2 // 2
    const = lambda j, b: (0, 0)
    once = pl.Buffered(1)
    in_specs = (
        [pl.BlockSpec((seq_len, HY_CB), (lambda j, b, g=g: (row_blk0 + b, g * ncb + j))) for g in range(3)]
        + [pl.BlockSpec((HY_SHORT, HY_CB), (lambda j, b, g=g: (0, g * ncb + j))) for g in range(3)]
        + [pl.BlockSpec((1, HY_CB), (lambda j, b, g=g: (0, g * ncb + j))) for g in range(3)]
        + [pl.BlockSpec((1, HY_CB), lambda j, b: (0, j))]
        + [pl.BlockSpec((1, n2, FFT_N1 * HY_CB), lambda j, b: (j, 0, 0), pipeline_mode=once)] * 2
        + [pl.BlockSpec((n2, half), const, pipeline_mode=once)] * 4
        + [pl.BlockSpec((n2 // chunk, half, chunk), lambda j, b: (0, 0, 0), pipeline_mode=once)] * 4
        + [pl.BlockSpec((n2, FFT_N1), const, pipeline_mode=once)] * 2
    )
    fwd = [a[:, :half] for a in (*consts["fwd_c"], *consts["fwd_s"])]
    args = [hyp, hyp, hyp, conv_w, conv_w, conv_w, conv_b, conv_b, conv_b, d_skip, gre, gim,
            *fwd, *consts["inv_c"], *consts["inv_s"], consts["tw_re"], consts["tw_im"]]
    return pl.pallas_call(
        functools.partial(_hyena_kernel, seq_len=seq_len, chunk=chunk),
        grid=(ncb, nb),
        in_specs=in_specs,
        out_specs=pl.BlockSpec((seq_len, HY_CB), lambda j, b: (b, j)),
        out_shape=jax.ShapeDtypeStruct((nb * seq_len, HY_W), BF16),
        scratch_shapes=[pltpu.VMEM((seq_len, HY_CB), F32), pltpu.VMEM((seq_len, HY_CB), F32),
                        pltpu.VMEM((half, FFT_N1 * HY_CB), F32)],
        compiler_params=_cparams(("arbitrary", "arbitrary")),
        name=f"hyena_mixer_{seq_len}",
    )(*args)


def _pack_pairs(h):
    half = h.shape[1] // 2
    lo = pltpu.bitcast(h[:, :half].astype(BF16).astype(F32), jnp.uint32) >> 16
    hi = pltpu.bitcast(h[:, half:].astype(BF16).astype(F32), jnp.uint32) & jnp.uint32(0xFFFF0000)
    return lo | hi


def _unpack_pairs(u):
    lo = pltpu.bitcast(u << 16, F32)
    hi = pltpu.bitcast(u & jnp.uint32(0xFFFF0000), F32)
    return jnp.concatenate([lo, hi], axis=1).astype(BF16)


def _mix_out_kernel(tmod_ref, x_ref, ya_ref, yh_ref, yn_ref, mod_ref, w_ref, g_ref, b_ref, rwh_ref, rwl_ref, rb_ref,
                    x1_out, h2_out, idx_out, wt_out, *, alpha, n_experts):
    del tmod_ref
    y = _dot(jnp.concatenate([ya_ref[...], yh_ref[...], yn_ref[...]], axis=1), w_ref[...])
    x1 = _layer_norm(alpha * x_ref[...] + mod_ref[0, 2:3, :] * y) * g_ref[...] + b_ref[...]
    x1_out[...] = x1
    h2 = _layer_norm(x1) * (1.0 + mod_ref[0, 4:5, :]) + mod_ref[0, 3:4, :]
    h2_out[...] = _pack_pairs(h2)
    h2_hi, h2_lo = _split_bf16(h2)
    logits = _dot3(h2_hi, h2_lo, rwh_ref[...], rwl_ref[...]) + rb_ref[...]
    lane = lax.broadcasted_iota(jnp.int32, logits.shape, 1)
    lane_f = lane.astype(F32)
    cur = jnp.where(lane < n_experts, logits, -jnp.inf)
    vals, idxs = [], []
    for _ in range(TOP_K):
        m = cur.max(axis=-1, keepdims=True)
        ix = jnp.where(cur == m, lane_f, float(LANES)).min(axis=-1, keepdims=True)
        vals.append(m)
        idxs.append(ix)
        cur = jnp.where(lane_f == ix, -jnp.inf, cur)
    es = [jnp.exp(v - vals[0]) for v in vals]
    denom = es[0]
    for e in es[1:]:
        denom = denom + e
    idx_v = jnp.zeros(logits.shape, F32)
    wt_v = jnp.zeros(logits.shape, F32)
    for k in range(TOP_K):
        idx_v = jnp.where(lane == k, idxs[k], idx_v)
        wt_v = jnp.where(lane == k, es[k] / denom, wt_v)
    idx_out[...] = idx_v.astype(jnp.int32)
    wt_out[...] = wt_v


def _mix_out(xf, y_att, y_hy, y_na, mods_l, w_out_bf, ln_g, ln_b, router_wp, router_bp, tile_mod, n_tok,
             alpha, n_experts):
    d = xf.shape[1]
    tm = TOKEN_TILE
    row = lambda i, tmod: (i, 0)
    const = lambda i, tmod: (0, 0)
    grid_spec = pltpu.PrefetchScalarGridSpec(
        num_scalar_prefetch=1,
        grid=(n_tok // tm,),
        in_specs=[
            pl.BlockSpec((tm, d), row),
            pl.BlockSpec((tm, ATT_W), row),
            pl.BlockSpec((tm, HY_W), row),
            pl.BlockSpec((tm, NA_W), row),
            pl.BlockSpec((1, 6, d), lambda i, tmod: (tmod[i], 0, 0)),
            pl.BlockSpec((D_MIX, d), const),
            pl.BlockSpec((1, d), const),
            pl.BlockSpec((1, d), const),
            pl.BlockSpec((d, LANES), const),
            pl.BlockSpec((d, LANES), const),
            pl.BlockSpec((1, LANES), const),
        ],
        out_specs=[
            pl.BlockSpec((tm, d), row),
            pl.BlockSpec((tm, d // 2), row),
            pl.BlockSpec((tm, LANES), row),
            pl.BlockSpec((tm, LANES), row),
        ],
    )
    return pl.pallas_call(
        functools.partial(_mix_out_kernel, alpha=alpha, n_experts=n_experts),
        grid_spec=grid_spec,
        out_shape=[
            jax.ShapeDtypeStruct((n_tok, d), F32),
            jax.ShapeDtypeStruct((n_tok, d // 2), jnp.uint32),
            jax.ShapeDtypeStruct((n_tok, LANES), jnp.int32),
            jax.ShapeDtypeStruct((n_tok, LANES), F32),
        ],
        compiler_params=_cparams(("arbitrary",)),
        name="mix_out_norm_route",
    )(tile_mod, xf, y_att, y_hy, y_na, mods_l, w_out_bf, ln_g, ln_b, *_split_bf16(router_wp), router_bp)


def _dispatch_kernel(slot_ref, h_ref, xs_in, xs_hbm, sem, *, tm):
    del xs_in

    def row_copy(src_row, dst_row):
        return pltpu.make_async_copy(h_ref.at[pl.ds(src_row, 1)], xs_hbm.at[pl.ds(dst_row, 1)], sem)

    def issue(t, carry):
        for k in range(TOP_K):
            row_copy(t, slot_ref[t * TOP_K + k]).start(priority=k % 2)
        return carry

    lax.fori_loop(0, tm, issue, 0, unroll=8)

    def drain(t, carry):
        for k in range(TOP_K):
            row_copy(0, 0).wait()
        return carry

    lax.fori_loop(0, tm, drain, 0, unroll=8)


def _dispatch(h2p, slots_flat, n_slots):
    n_tok, hw = h2p.shape
    tm = TOKEN_TILE
    xs0 = jnp.zeros((n_slots, hw), h2p.dtype)
    return pl.pallas_call(
        functools.partial(_dispatch_kernel, tm=tm),
        grid=(n_tok // tm,),
        in_specs=[
            pl.BlockSpec((tm * TOP_K,), lambda i: (i,), memory_space=pltpu.SMEM),
            pl.BlockSpec((tm, hw), lambda i: (i, 0)),
            pl.BlockSpec(memory_space=pl.ANY),
        ],
        out_specs=pl.BlockSpec(memory_space=pl.ANY),
        out_shape=jax.ShapeDtypeStruct((n_slots, hw), h2p.dtype),
        scratch_shapes=[pltpu.SemaphoreType.DMA],
        input_output_aliases={2: 0},
        compiler_params=_cparams(("arbitrary",)),
        name="moe_dispatch",
    )(slots_flat, h2p, xs0)


def _expert_kernel(te_ref, nu_ref, x_ref, w1_ref, b1_ref, w2_ref, b2_ref, o_ref, w1_bf, w2_bf, *, d_expert):
    i = pl.program_id(0)

    @pl.when((i == 0) | (te_ref[i] != te_ref[jnp.maximum(i - 1, 0)]))
    def _():
        w1_bf[...] = w1_ref[0].astype(BF16)
        w2_bf[...] = w2_ref[0].astype(BF16)

    @pl.when(i < nu_ref[0])
    def _():
        x = _unpack_pairs(x_ref[...])
        cw = min(EXPERT_HIDDEN_CHUNK, d_expert)
        acc = None
        for c0 in range(0, d_expert, cw):
            glu = _dot(x, w1_bf[:, c0:c0 + cw]) + b1_ref[0, :, c0:c0 + cw]
            lin = _dot(x, w1_bf[:, d_expert + c0:d_expert + c0 + cw]) + b1_ref[0, :, d_expert + c0:d_expert + c0 + cw]
            glu = jnp.minimum(glu, SWIGLU_LIMIT)
            lin = jnp.clip(lin, -SWIGLU_LIMIT, SWIGLU_LIMIT)
            act = glu * jax.nn.sigmoid(SWIGLU_ALPHA * glu) * (lin + 1.0)
            part = _dot(act.astype(BF16), w2_bf[c0:c0 + cw, :])
            acc = part if acc is None else acc + part
        o_ref[...] = acc + b2_ref[0]

    @pl.when(i >= nu_ref[0])
    def _():
        o_ref[...] = jnp.zeros(o_ref.shape, o_ref.dtype)


def _experts(xs, tile_expert, n_used, w1, b1, w2, b2, layer):
    n_slots, hw = xs.shape
    n_exp = b1.shape[0]
    _, d, d2 = w1.shape
    d_expert = d2 // 2
    tm = MOE_TILE
    n_tiles = n_slots // tm
    e0 = layer * n_exp
    grid_spec = pltpu.PrefetchScalarGridSpec(
        num_scalar_prefetch=2,
        grid=(n_tiles,),
        in_specs=[
            pl.BlockSpec((tm, hw), lambda i, te, nu: (i, 0)),
            pl.BlockSpec((1, d, d2), lambda i, te, nu: (e0 + te[i], 0, 0)),
            pl.BlockSpec((1, 1, d2), lambda i, te, nu: (te[i], 0, 0)),
            pl.BlockSpec((1, d_expert, d), lambda i, te, nu: (e0 + te[i], 0, 0)),
            pl.BlockSpec((1, 1, d), lambda i, te, nu: (te[i], 0, 0)),
        ],
        out_specs=pl.BlockSpec((tm, d), lambda i, te, nu: (i, 0)),
        scratch_shapes=[pltpu.VMEM((d, d2), BF16), pltpu.VMEM((d_expert, d), BF16)],
    )
    return pl.pallas_call(
        functools.partial(_expert_kernel, d_expert=d_expert),
        grid_spec=grid_spec,
        out_shape=jax.ShapeDtypeStruct((n_slots, d), F32),
        compiler_params=_cparams(("arbitrary",)),
        name="moe_experts",
    )(tile_expert, n_used, xs, w1, b1.reshape(n_exp, 1, d2), w2, b2.reshape(n_exp, 1, d))


def _combine_kernel(tmod_ref, slot_ref, next_ref, x1_ref, wt_ref, mod_ref, g_ref, b_ref, ys_hbm, o_ref, buf, sem,
                    *, tm, alpha):
    del tmod_ref
    i = pl.program_id(0)

    def row_copy(src_row, b, k, t):
        return pltpu.make_async_copy(ys_hbm.at[pl.ds(src_row, 1)], buf.at[b, k, pl.ds(t, 1)], sem.at[b])

    def issue(ref, off, b):
        def body(t, carry):
            for k in range(TOP_K):
                row_copy(ref[off + t * TOP_K + k], b, k, t).start(priority=k % 2)
            return carry

        lax.fori_loop(0, tm, body, 0, unroll=8)

    def drain(b):
        def body(t, carry):
            for k in range(TOP_K):
                row_copy(0, b, k, t).wait()
            return carry

        lax.fori_loop(0, tm, body, 0, unroll=8)

    def finish(b):
        rows = pl.ds(b * tm, tm)
        wt = wt_ref[rows, :]
        moe = buf[b, 0] * wt[:, 0:1]
        for k in range(1, TOP_K):
            moe = moe + buf[b, k] * wt[:, k:k + 1]
        o_ref[rows, :] = _layer_norm(alpha * x1_ref[rows, :] + mod_ref[0, 5:6, :] * moe) * g_ref[...] + b_ref[...]

    @pl.when(i == 0)
    def _():
        issue(slot_ref, 0, 0)

    issue(slot_ref, tm * TOP_K, 1)
    drain(0)
    finish(0)

    @pl.when(i + 1 < pl.num_programs(0))
    def _():
        issue(next_ref, 0, 0)

    drain(1)
    finish(1)


def _combine(x1, wts, slots_flat, ys, mods_l, ln_g, ln_b, tile_mod, alpha):
    n_tok, d = x1.shape
    tm = TOKEN_TILE
    n_steps = n_tok // (2 * tm)
    assert n_tok % (2 * tm) == 0
    row = lambda i, tmod: (i, 0)
    const = lambda i, tmod: (0, 0)
    grid_spec = pltpu.PrefetchScalarGridSpec(
        num_scalar_prefetch=1,
        grid=(n_steps,),
        in_specs=[
            pl.BlockSpec((2 * tm * TOP_K,), lambda i, tmod: (i,), memory_space=pltpu.SMEM),
            pl.BlockSpec((tm * TOP_K,), lambda i, tmod: (jnp.minimum(2 * i + 2, 2 * n_steps - 1),),
                         memory_space=pltpu.SMEM),
            pl.BlockSpec((2 * tm, d), row),
            pl.BlockSpec((2 * tm, LANES), row),
            pl.BlockSpec((1, 6, d), lambda i, tmod: (tmod[2 * i], 0, 0)),
            pl.BlockSpec((1, d), const),
            pl.BlockSpec((1, d), const),
            pl.BlockSpec(memory_space=pl.ANY),
        ],
        out_specs=pl.BlockSpec((2 * tm, d), row),
        scratch_shapes=[pltpu.VMEM((2, TOP_K, tm, d), F32), pltpu.SemaphoreType.DMA((2,))],
    )
    return pl.pallas_call(
        functools.partial(_combine_kernel, tm=tm, alpha=alpha),
        grid_spec=grid_spec,
        out_shape=jax.ShapeDtypeStruct((n_tok, d), F32),
        compiler_params=_cparams(("arbitrary",)),
        name="moe_combine_norm",
    )(tile_mod, slots_flat, slots_flat, x1, wts, mods_l, ln_g, ln_b, ys)


def _routing_tables(idx4, n_experts, tile):
    n_tok = idx4.shape[0]
    blk = TOKEN_TILE
    hit = idx4[:, :, None] == jnp.arange(n_experts, dtype=jnp.int32)[None, None, :]
    onehot = hit.any(axis=1)
    blocks = onehot.reshape(n_tok // blk, blk, n_experts).astype(BF16)
    tri = jnp.asarray(np.tril(np.ones((blk, blk), np.float32), -1), BF16)
    within = jnp.einsum("ij,bjk->bik", tri, blocks, preferred_element_type=F32).astype(jnp.int32)
    block_tot = onehot.reshape(n_tok // blk, blk, n_experts).sum(axis=1, dtype=jnp.int32)
    block_pre = jnp.cumsum(block_tot, axis=0) - block_tot
    pos = (within + block_pre[:, None, :]).reshape(n_tok, n_experts)
    counts = block_tot.sum(axis=0)
    padded = ((counts + tile - 1) // tile) * tile
    ends = jnp.cumsum(padded)
    offs = ends - padded
    slots = jnp.sum(jnp.where(hit, (offs[None, :] + pos)[:, None, :], 0), axis=2)
    n_tiles = (n_tok * TOP_K) // tile + n_experts
    tile_start = jnp.arange(n_tiles, dtype=jnp.int32) * tile
    tile_expert = jnp.minimum(jnp.sum(tile_start[:, None] >= ends[None, :], axis=1), n_experts - 1).astype(jnp.int32)
    n_used = (ends[-1] // tile).astype(jnp.int32)
    last = tile_expert[jnp.maximum(n_used - 1, 0)]
    tile_expert = jnp.where(jnp.arange(n_tiles) < n_used, tile_expert, last)
    return slots.astype(jnp.int32).reshape(-1), tile_expert, n_used.reshape(1), n_tiles * tile


def _rope_tables(n, grid_w, extra_rows):
    t = np.arange(n)
    row = (t // grid_w).astype(np.float32)[:, None]
    col = (t % grid_w).astype(np.float32)[:, None]
    axis_dim = HEAD_DIM // 2
    inv_freq = (ROPE_THETA ** (-np.arange(0, axis_dim, 2, dtype=np.float32) / axis_dim)).astype(np.float32)
    ang_r = row * inv_freq
    ang_c = col * inv_freq
    ang = np.concatenate([ang_r, ang_r, ang_c, ang_c], axis=-1)
    cos = np.cos(ang).astype(np.float32)
    sin = np.sin(ang).astype(np.float32)
    sign = np.where((np.arange(HEAD_DIM) % 32) < 16, -1.0, 1.0).astype(np.float32)
    sin = sin * sign[None, :]
    cos = np.concatenate([cos, np.ones((extra_rows, HEAD_DIM), np.float32)], axis=0)
    sin = np.concatenate([sin, np.zeros((extra_rows, HEAD_DIM), np.float32)], axis=0)
    reps = LANES // HEAD_DIM
    return jnp.asarray(np.tile(cos, (1, reps))), jnp.asarray(np.tile(sin, (1, reps)))


def _forward(x, c, ctx, c_ctx, ada_w, ada_b, w_in, w_out, q_gain, k_gain, hy_conv_w, hy_conv_b,
             hy_w1, hy_b1, hy_freq, hy_w2, hy_b2, hy_w3, hy_b3, hy_d, na_rpb, ln1_g, ln1_b,
             ln2_g, ln2_b, router_w, router_b, exp_w1, exp_b1, exp_w2, exp_b2, *, grid_w):
    nb, n, d = x.shape
    m = ctx.shape[1]
    depth = ada_w.shape[0]
    n_experts = router_w.shape[-1]
    alpha = (2.0 * depth) ** 0.25
    tm = TOKEN_TILE
    assert n % tm == 0 and m % tm == 0 and n % m == 0 and n_experts <= LANES
    assert (n // tm) % 2 == 0 and (nb * m // tm) % 2 == 0
    t_lat, t_all = nb * n, nb * (n + m)

    xf = jnp.concatenate([x.reshape(nb * n, d), ctx.reshape(nb * m, d)], axis=0)
    cvec = jnp.zeros((SUBLANES, d), F32).at[:nb].set(c).at[nb].set(c_ctx)
    mods = _adaln(cvec, ada_w, ada_b).reshape(depth, SUBLANES, 6, d)

    tiles = np.arange(t_all // tm)
    lat_tile = tiles < t_lat // tm
    tile_mod = jnp.asarray(np.where(lat_tile, tiles // (n // tm), nb), jnp.int32)
    tile_pos = jnp.asarray(np.where(lat_tile, tiles % (n // tm), n // tm), jnp.int32)
    cos_t, sin_t = _rope_tables(n, grid_w, tm)
    gmat = jnp.asarray(np.kron(np.eye(ATT_Q_HEADS), np.ones((HEAD_DIM, HEAD_DIM))), BF16)
    dft_lat = _dft_constants(n)
    dft_ctx = _dft_constants(m)

    for l in range(depth):
        need_ctx = l < depth - 1
        n_tok = t_all if need_ctx else t_lat
        q, k, v, hyp, nq, nk, nv = _proj(
            xf, mods[l], w_in[l].astype(BF16), jnp.tile(q_gain[l], ATT_Q_HEADS)[None, :],
            jnp.tile(k_gain[l], ATT_KV_HEADS)[None, :], gmat, cos_t, sin_t, tile_mod, tile_pos)
        y_att = _gqa(q, k, v, nb, n, m, need_ctx)
        filt = (hy_w1[l], hy_b1[l], hy_freq[l], hy_w2[l], hy_b2[l], hy_w3[l], hy_b3[l])
        gre, gim = _hyena_filter_spectrum(n, dft_lat, *filt)
        cb = hy_conv_b[l][None, :]
        dsk = hy_d[l][None, :]
        y_hy = _hyena(hyp, hy_conv_w[l], cb, dsk, gre, gim, dft_lat, nb, n, 0)
        if need_ctx:
            gre_c, gim_c = _hyena_filter_spectrum(m, dft_ctx, *filt)
            y_hy_c = _hyena(hyp, hy_conv_w[l], cb, dsk, gre_c, gim_c, dft_ctx, nb, m, t_lat // m)
            y_hy = jnp.concatenate([y_hy, y_hy_c], axis=0)
        bias_tab = _na_bias_table(na_rpb[l], grid_w, n // grid_w)
        y_na = _na(nq, nk, nv, bias_tab, nb, n, m, grid_w, need_ctx)

        router_wp = jnp.zeros((d, LANES), F32).at[:, :n_experts].set(router_w[l])
        router_bp = jnp.zeros((1, LANES), F32).at[0, :n_experts].set(router_b[l])
        x1, h2p, idx, wts = _mix_out(xf, y_att, y_hy, y_na, mods[l], w_out[l].astype(BF16), ln1_g[l][None, :],
                                     ln1_b[l][None, :], router_wp, router_bp, tile_mod, n_tok, alpha, n_experts)
        slots, tile_expert, n_used, n_slots = _routing_tables(idx[:, :TOP_K], n_experts, MOE_TILE)
        xs = _dispatch(h2p, slots, n_slots)
        ys = _experts(xs, tile_expert, n_used, exp_w1.reshape((-1,) + exp_w1.shape[2:]), exp_b1[l],
                      exp_w2.reshape((-1,) + exp_w2.shape[2:]), exp_b2[l], l)
        xf = _combine(x1, wts, slots, ys, mods[l], ln2_g[l][None, :], ln2_b[l][None, :], tile_mod, alpha)
    return xf[:t_lat].reshape(nb, n, d)


def kernel(x, c, ctx, c_ctx, ada_w, ada_b, w_in, w_out, q_gain, k_gain, hy_conv_w, hy_conv_b, hy_w1, hy_b1,
           hy_freq, hy_w2, hy_b2, hy_w3, hy_b3, hy_d, na_rpb, ln1_g, ln1_b, ln2_g, ln2_b, router_w, router_b,
           exp_w1, exp_b1, exp_w2, exp_b2):
    return _forward(x, c, ctx, c_ctx, ada_w, ada_b, w_in, w_out, q_gain, k_gain, hy_conv_w, hy_conv_b, hy_w1,
                    hy_b1, hy_freq, hy_w2, hy_b2, hy_w3, hy_b3, hy_d, na_rpb, ln1_g, ln1_b, ln2_g, ln2_b,
                    router_w, router_b, exp_w1, exp_b1, exp_w2, exp_b2, grid_w=64)
```

```python
import cmath
import functools
import math

import jax
import jax.numpy as jnp
import numpy as np
from jax import lax
from jax.experimental import pallas as pl
from jax.experimental.pallas import tpu as pltpu

F32 = jnp.float32
BF16 = jnp.bfloat16
HIGHEST = lax.Precision.HIGHEST

HEAD_DIM = 64
ATT_Q_HEADS = 8
ATT_KV_HEADS = 2
ATT_GROUP = ATT_Q_HEADS // ATT_KV_HEADS
ATT_W = ATT_Q_HEADS * HEAD_DIM
KV_W = ATT_KV_HEADS * HEAD_DIM
HY_W = 256
HY_SHORT = 3
HY_EMB = 33
HY_BANDS = (HY_EMB - 1) // 2
HY_FO = 64
HY_MIN_DECAY = math.log(1e-2) / 1.5
HY_MAX_DECAY = math.log(1e-2) / 0.3
NA_HEADS = 4
NA_W = NA_HEADS * HEAD_DIM
NA_WIN_ROWS = 8
NA_WIN_COLS = 16
D_MIX = ATT_W + HY_W + NA_W
D_IN = ATT_W + 2 * KV_W + 3 * HY_W + 3 * NA_W
TOP_K = 4
SWIGLU_ALPHA = 1.702
SWIGLU_LIMIT = 7.0
ROPE_THETA = 10000.0
NORM_EPS = 1e-6
LOG2E = 1.0 / math.log(2.0)
Q_SCALE = HEAD_DIM ** -0.5 * LOG2E

LANES = 128
SUBLANES = 8
NEG_BIG = -1e30

TOKEN_TILE = 256
ATT_Q_TILE = 256
NA_Q_ROWS = 8
NA_K_ROWS = 16
FFT_N1 = 8
HY_CB = 128
FREQ_CHUNK = 256
MOE_TILE = 512
VMEM_LIMIT = 56 * 1024 * 1024


def _cparams(sem, vmem=VMEM_LIMIT):
    return pltpu.CompilerParams(dimension_semantics=sem, vmem_limit_bytes=vmem)


def _layer_norm(x):
    mu = jnp.mean(x, axis=-1, keepdims=True)
    xc = x - mu
    var = jnp.mean(xc * xc, axis=-1, keepdims=True)
    return xc * lax.rsqrt(var + NORM_EPS)


def _dot(a, b):
    return jnp.dot(a, b, preferred_element_type=F32)


def _dot_nt(a, b):
    return lax.dot_general(a, b, (((1,), (1,)), ((), ())), preferred_element_type=F32)


def _split_bf16(x):
    hi = x.astype(BF16)
    lo = (x - hi.astype(F32)).astype(BF16)
    return hi, lo


def _dot3(a_hi, a_lo, b_hi, b_lo):
    return _dot(a_hi, b_hi) + _dot(a_hi, b_lo) + _dot(a_lo, b_hi)


def _adaln_kernel(c_ref, w_ref, b_ref, o_ref):
    c = c_ref[...]
    a = c * jax.nn.sigmoid(c)
    o_ref[0] = jnp.dot(a, w_ref[0], preferred_element_type=F32, precision=HIGHEST) + b_ref[0]


def _adaln(cvec, ada_w, ada_b):
    n_layers, d, d6 = ada_w.shape
    bn = d6 // 4 if (d6 // 4) % LANES == 0 else d6
    return pl.pallas_call(
        _adaln_kernel,
        grid=(n_layers, d6 // bn),
        in_specs=[
            pl.BlockSpec((SUBLANES, d), lambda l, j: (0, 0)),
            pl.BlockSpec((1, d, bn), lambda l, j: (l, 0, j)),
            pl.BlockSpec((1, 1, bn), lambda l, j: (l, 0, j)),
        ],
        out_specs=pl.BlockSpec((1, SUBLANES, bn), lambda l, j: (l, 0, j)),
        out_shape=jax.ShapeDtypeStruct((n_layers, SUBLANES, d6), F32),
        compiler_params=_cparams(("arbitrary", "arbitrary")),
        name="adaln",
    )(cvec, ada_w, ada_b.reshape(n_layers, 1, d6))


def _head_norm(a, g_ref, gain):
    w = a.shape[1]
    sq = a * a
    hi, lo = _split_bf16(sq)
    g = g_ref[0:w, 0:w]
    ss = _dot(hi, g) + _dot(lo, g)
    return a * lax.rsqrt(ss * (1.0 / HEAD_DIM) + NORM_EPS) * gain


def _rope(a, cos, sin_signed):
    w = a.shape[1]
    reps = w // LANES
    c = jnp.concatenate([cos] * reps, axis=1) if reps > 1 else cos
    s = jnp.concatenate([sin_signed] * reps, axis=1) if reps > 1 else sin_signed
    lane = lax.broadcasted_iota(jnp.int32, a.shape, 1)
    first = (lane & 31) < 16
    rot = jnp.where(first, pltpu.roll(a, w - 16, axis=1), pltpu.roll(a, 16, axis=1))
    return a * c + rot * s


def _proj_kernel(tmod_ref, tpos_ref, x_ref, mod_ref, w_ref, qg_ref, kg_ref, g_ref, cos_ref, sin_ref,
                 q_out, k_out, v_out, hy_out, nq_out, nk_out, nv_out):
    del tmod_ref, tpos_ref
    h = _layer_norm(x_ref[...]) * (1.0 + mod_ref[0, 1:2, :]) + mod_ref[0, 0:1, :]
    p = _dot(h.astype(BF16), w_ref[...])
    o = 0
    aq = p[:, o:o + ATT_W]; o += ATT_W
    ak = p[:, o:o + KV_W]; o += KV_W
    av = p[:, o:o + KV_W]; o += KV_W
    hy = p[:, o:o + 3 * HY_W]; o += 3 * HY_W
    nq = p[:, o:o + NA_W]; o += NA_W
    nk = p[:, o:o + NA_W]; o += NA_W
    nv = p[:, o:o + NA_W]
    cos = cos_ref[...]
    sin = sin_ref[...]
    q = _rope(_head_norm(aq, g_ref, qg_ref[...]), cos, sin) * Q_SCALE
    k = _rope(_head_norm(ak, g_ref, kg_ref[...]), cos, sin)
    q_out[...] = q.astype(BF16)
    k_out[...] = k.astype(BF16)
    v_out[...] = av.astype(BF16)
    hy_out[...] = hy
    nq_out[...] = (nq * Q_SCALE).astype(BF16)
    nk_out[...] = nk.astype(BF16)
    nv_out[...] = nv.astype(BF16)


def _proj(xf, mods_l, w_in_bf, q_gain, k_gain, gmat, cos_t, sin_t, tile_mod, tile_pos):
    t, d = xf.shape
    tm = TOKEN_TILE
    n_tiles = t // tm
    row = lambda i, tmod, tpos: (i, 0)
    const = lambda i, tmod, tpos: (0, 0)
    widths = (ATT_W, KV_W, KV_W, 3 * HY_W, NA_W, NA_W, NA_W)
    dtypes = (BF16, BF16, BF16, F32, BF16, BF16, BF16)
    grid_spec = pltpu.PrefetchScalarGridSpec(
        num_scalar_prefetch=2,
        grid=(n_tiles,),
        in_specs=[
            pl.BlockSpec((tm, d), row),
            pl.BlockSpec((1, 6, d), lambda i, tmod, tpos: (tmod[i], 0, 0)),
            pl.BlockSpec((d, D_IN), const),
            pl.BlockSpec((1, ATT_W), const),
            pl.BlockSpec((1, KV_W), const),
            pl.BlockSpec((ATT_W, ATT_W), const),
            pl.BlockSpec((tm, LANES), lambda i, tmod, tpos: (tpos[i], 0)),
            pl.BlockSpec((tm, LANES), lambda i, tmod, tpos: (tpos[i], 0)),
        ],
        out_specs=[pl.BlockSpec((tm, w), row) for w in widths],
    )
    return pl.pallas_call(
        _proj_kernel,
        grid_spec=grid_spec,
        out_shape=[jax.ShapeDtypeStruct((t, w), dt) for w, dt in zip(widths, dtypes)],
        compiler_params=_cparams(("arbitrary",)),
        name="ln_mod_proj",
    )(tile_mod, tile_pos, xf, mods_l, w_in_bf, q_gain, k_gain, gmat, cos_t, sin_t)


def _attend(q, segs, n_kv, grp, bias=None):
    tq = q.shape[0]
    hd = HEAD_DIM
    outs = []
    for j in range(n_kv):
        heads = [q[:, (j * grp + g) * hd:(j * grp + g + 1) * hd] for g in range(grp)]
        qs = jnp.concatenate(heads, axis=0) if grp > 1 else heads[0]
        scores = []
        for si, (k, _) in enumerate(segs):
            s = _dot_nt(qs, k[:, j * hd:(j + 1) * hd])
            if bias is not None and bias[j][si] is not None:
                s = s + bias[j][si]
            scores.append(s)
        m = scores[0].max(axis=-1, keepdims=True)
        for s in scores[1:]:
            m = jnp.maximum(m, s.max(axis=-1, keepdims=True))
        denom = None
        acc = None
        for s, (_, v) in zip(scores, segs):
            p = jnp.exp2(s - m)
            ps = p.sum(axis=-1, keepdims=True)
            pv = _dot(p.astype(BF16), v[:, j * hd:(j + 1) * hd])
            denom = ps if denom is None else denom + ps
            acc = pv if acc is None else acc + pv
        o = acc / denom
        for g in range(grp):
            outs.append(o[g * tq:(g + 1) * tq])
    return jnp.concatenate(outs, axis=1)


def _gqa_kernel(q_ref, kl_ref, vl_ref, kc_ref, vc_ref, o_ref):
    segs = [(kl_ref[...], vl_ref[...]), (kc_ref[...], vc_ref[...])]
    o_ref[...] = _attend(q_ref[...], segs, ATT_KV_HEADS, ATT_GROUP).astype(o_ref.dtype)


def _gqa(q, k, v, nb, n, m, with_ctx):
    t = q.shape[0]
    tq = ATT_Q_TILE
    n_lat = n // tq
    ctx_blk0 = nb * n // m
    qmap = lambda b, i: (b * n_lat + i, 0)
    lat = lambda b, i: (b, 0)
    ctx = lambda b, i: (ctx_blk0 + b, 0)
    y = pl.pallas_call(
        _gqa_kernel,
        grid=(nb, n_lat),
        in_specs=[
            pl.BlockSpec((tq, ATT_W), qmap),
            pl.BlockSpec((n, KV_W), lat),
            pl.BlockSpec((n, KV_W), lat),
            pl.BlockSpec((m, KV_W), ctx),
            pl.BlockSpec((m, KV_W), ctx),
        ],
        out_specs=pl.BlockSpec((tq, ATT_W), qmap),
        out_shape=jax.ShapeDtypeStruct((nb * n, ATT_W), BF16),
        compiler_params=_cparams(("arbitrary", "arbitrary")),
        name="gqa_attention",
    )(q, k, v, k, v)
    if not with_ctx:
        return y
    yc = _ctx_attention(q, k, v, nb, n, m, ATT_KV_HEADS, ATT_GROUP, ATT_W, "gqa_ctx_attention")
    return jnp.concatenate([y, yc], axis=0)


def _na_kernel(q_ref, kl_ref, vl_ref, kc_ref, vc_ref, bias_ref, o_ref, *, grid_w, grid_rows):
    i = pl.program_id(1)
    krow0 = jnp.clip(i * NA_Q_ROWS - NA_WIN_ROWS // 2, 0, grid_rows - NA_K_ROWS)
    start = pl.multiple_of(krow0 * grid_w, 4 * grid_w)
    nk = NA_K_ROWS * grid_w
    segs = [(kl_ref[pl.ds(start, nk), :], vl_ref[pl.ds(start, nk), :]), (kc_ref[...], vc_ref[...])]
    bias = [[bias_ref[0, h], None] for h in range(NA_HEADS)]
    o_ref[...] = _attend(q_ref[...], segs, NA_HEADS, 1, bias).astype(o_ref.dtype)


def _na_bias_table(rpb, grid_w, grid_rows):
    qr, kr = NA_Q_ROWS, NA_K_ROWS
    n_rr, n_rc = 2 * NA_WIN_ROWS - 1, 2 * NA_WIN_COLS - 1
    c = np.arange(grid_w)[:, None]
    kcol = np.arange(grid_w)[None, :]
    cs = np.clip(c - NA_WIN_COLS // 2, 0, grid_w - NA_WIN_COLS)
    col_ok = (kcol >= cs) & (kcol < cs + NA_WIN_COLS)
    csel = ((kcol - c + NA_WIN_COLS - 1)[:, :, None] == np.arange(n_rc)) & col_ok[:, :, None]
    rsel, mask = [], []
    for r0 in (0, qr, grid_rows - qr):
        k0 = int(np.clip(r0 - NA_WIN_ROWS // 2, 0, grid_rows - kr))
        r = r0 + np.arange(qr)[:, None]
        krow = k0 + np.arange(kr)[None, :]
        rs = np.clip(r - NA_WIN_ROWS // 2, 0, grid_rows - NA_WIN_ROWS)
        row_ok = (krow >= rs) & (krow < rs + NA_WIN_ROWS)
        rsel.append(((krow - r + NA_WIN_ROWS - 1)[:, :, None] == np.arange(n_rr)) & row_ok[:, :, None])
        ok = row_ok[:, None, :, None] & col_ok[None, :, None, :]
        mask.append(np.where(ok, 0.0, NEG_BIG))
    rsel = jnp.asarray(np.stack(rsel), F32)
    mask = jnp.asarray(np.stack(mask), F32)
    vals = jnp.einsum("tjia,hab,ckb->thjcik", rsel, rpb, jnp.asarray(csel, F32), precision=HIGHEST)
    return (vals * LOG2E + mask[:, None]).reshape(3, NA_HEADS, qr * grid_w, kr * grid_w)


def _na(nq, nk, nv, bias_tab, nb, n, m, grid_w, with_ctx):
    t = nq.shape[0]
    grid_rows = n // grid_w
    tq = NA_Q_ROWS * grid_w
    n_lat = n // tq
    ctx_blk0 = nb * n // m
    lat = lambda b, i: (b, 0)
    ctx = lambda b, i: (ctx_blk0 + b, 0)
    qmap = lambda b, i: (b * n_lat + i, 0)

    def bmap(b, i):
        return (jnp.where(i == 0, 0, jnp.where(i >= n_lat - 1, 2, 1)), 0, 0, 0)

    y = pl.pallas_call(
        functools.partial(_na_kernel, grid_w=grid_w, grid_rows=grid_rows),
        grid=(nb, n_lat),
        in_specs=[
            pl.BlockSpec((tq, NA_W), qmap),
            pl.BlockSpec((n, NA_W), lat),
            pl.BlockSpec((n, NA_W), lat),
            pl.BlockSpec((m, NA_W), ctx),
            pl.BlockSpec((m, NA_W), ctx),
            pl.BlockSpec((1, NA_HEADS, tq, NA_K_ROWS * grid_w), bmap),
        ],
        out_specs=pl.BlockSpec((tq, NA_W), qmap),
        out_shape=jax.ShapeDtypeStruct((nb * n, NA_W), BF16),
        compiler_params=_cparams(("arbitrary", "arbitrary")),
        name="neighbourhood_attention",
    )(nq, nk, nv, nk, nv, bias_tab)
    if not with_ctx:
        return y
    yc = _ctx_attention(nq, nk, nv, nb, n, m, NA_HEADS, 1, NA_W, "na_ctx_attention")
    return jnp.concatenate([y, yc], axis=0)


def _ctx_attn_kernel(q_ref, k_ref, v_ref, o_ref, *, n_kv, grp):
    segs = [(k_ref[...], v_ref[...])]
    o_ref[...] = _attend(q_ref[...], segs, n_kv, grp).astype(o_ref.dtype)


def _ctx_attention(q, k, v, nb, n, m, n_kv, grp, width, name):
    ctx_blk0 = nb * n // m
    ctx = lambda b: (ctx_blk0 + b, 0)
    kw = n_kv * HEAD_DIM
    return pl.pallas_call(
        functools.partial(_ctx_attn_kernel, n_kv=n_kv, grp=grp),
        grid=(nb,),
        in_specs=[
            pl.BlockSpec((m, width), ctx),
            pl.BlockSpec((m, kw), ctx),
            pl.BlockSpec((m, kw), ctx),
        ],
        out_specs=pl.BlockSpec((m, width), lambda b: (b, 0)),
        out_shape=jax.ShapeDtypeStruct((nb * m, width), BF16),
        compiler_params=_cparams(("arbitrary",)),
        name=name,
    )(q, k, v)


def _dft_constants(seq_len):
    n = 2 * seq_len
    n2 = n // FFT_N1
    idx = np.arange(n2, dtype=np.int64)
    ang = 2.0 * np.pi * ((idx[:, None] * idx[None, :]) % n2).astype(np.float64) / n2
    cmat, smat = np.cos(ang), np.sin(ang)

    def split(a):
        hi = jnp.asarray(a, F32).astype(BF16)
        lo = (jnp.asarray(a, F32) - hi.astype(F32)).astype(BF16)
        return hi, lo

    def col_chunks(a):
        return np.transpose(a.reshape(a.shape[0], n2 // chunk, chunk), (1, 0, 2))

    chunk = min(FREQ_CHUNK, n2)
    tw_ang = 2.0 * np.pi * (idx[:, None] * np.arange(FFT_N1)[None, :]).astype(np.float64) / n
    return dict(
        n2=n2, chunk=chunk,
        fwd_c=split(cmat), fwd_s=split(smat),
        inv_c=split(col_chunks(cmat[: n2 // 2] / n)),
        inv_s=split(col_chunks(smat[: n2 // 2] / n)),
        tw_re=jnp.asarray(np.cos(tw_ang), F32), tw_im=jnp.asarray(-np.sin(tw_ang), F32),
    )


def _cmul_const(a, w):
    re, im = a
    if abs(w - 1) < 1e-12:
        return a
    if abs(w + 1) < 1e-12:
        return (-re, -im)
    if abs(w + 1j) < 1e-12:
        return (im, -re)
    if abs(w - 1j) < 1e-12:
        return (-im, re)
    return (re * w.real - im * w.imag, re * w.imag + im * w.real)


def _fft_blocks(xs, sign):
    n = len(xs)
    if n == 1:
        return xs
    even = _fft_blocks(xs[0::2], sign)
    odd = _fft_blocks(xs[1::2], sign)
    out = [None] * n
    for k in range(n // 2):
        t = _cmul_const(odd[k], cmath.exp(sign * 2j * cmath.pi * k / n))
        out[k] = (even[k][0] + t[0], even[k][1] + t[1])
        out[k + n // 2] = (even[k][0] - t[0], even[k][1] - t[1])
    return out


def _lane_blocks(a):
    return [a[:, j * LANES:(j + 1) * LANES] for j in range(a.shape[1] // LANES)]


def _forward_spectrum(xh, xl, rows, ch_ref, cl_ref, sh_ref, sl_ref, twr_ref, twi_ref):
    a_re = _dot3(ch_ref[rows, :], cl_ref[rows, :], xh, xl)
    a_im = -_dot3(sh_ref[rows, :], sl_ref[rows, :], xh, xl)
    blocks = []
    for s1, (re, im) in enumerate(zip(_lane_blocks(a_re), _lane_blocks(a_im))):
        tr = twr_ref[rows, s1:s1 + 1]
        ti = twi_ref[rows, s1:s1 + 1]
        blocks.append((re * tr - im * ti, re * ti + im * tr))
    return _fft_blocks(blocks, -1)


def _filter_kernel(emb_ref, w1_ref, b1_ref, fr_ref, w2_ref, b2_ref, w3f_ref, w3b_ref, b3f_ref, b3b_ref,
                   dec_ref, ch_ref, cl_ref, sh_ref, sl_ref, twr_ref, twi_ref, gre_ref, gim_ref, g_scr, hid_scr, *,
                   seq_len, chunk):
    n = 2 * seq_len
    n2 = n // FFT_N1
    def mm(a, w_ref):
        return _dot3(*_split_bf16(a), *_split_bf16(w_ref[...]))

    @pl.when(pl.program_id(0) == 0)
    def _():
        hid1 = jnp.sin(fr_ref[0:1, :] * (mm(emb_ref[...], w1_ref) + b1_ref[...]))
        hid_scr[...] = jnp.sin(fr_ref[1:2, :] * (mm(hid1, w2_ref) + b2_ref[...]))

    hid = hid_scr[...]
    f_fwd = mm(hid, w3f_ref) + b3f_ref[...]
    f_bwd = mm(hid, w3b_ref) + b3b_ref[...]
    row = lax.broadcasted_iota(jnp.int32, f_fwd.shape, 0)
    t_pos = emb_ref[:, 0:1]
    g = jnp.where(row < seq_len, f_fwd, jnp.where(row > seq_len, f_bwd, 0.0))
    g = g * jnp.exp(-t_pos * jnp.abs(dec_ref[...]))
    g = g * lax.rsqrt(jnp.sum(g * g, axis=0, keepdims=True) + NORM_EPS)
    g_scr[...] = g
    x = jnp.concatenate([g_scr[pl.ds(s1, n2, stride=FFT_N1), :] for s1 in range(FFT_N1)], axis=1)
    xh, xl = _split_bf16(x)

    def freq_chunk(ci, carry):
        rows = pl.ds(pl.multiple_of(ci * chunk, chunk), chunk)
        spec = _forward_spectrum(xh, xl, rows, ch_ref, cl_ref, sh_ref, sl_ref, twr_ref, twi_ref)
        gre_ref[0, rows, :] = jnp.concatenate([b[0] for b in spec], axis=1)
        gim_ref[0, rows, :] = jnp.concatenate([b[1] for b in spec], axis=1)
        return carry

    lax.fori_loop(0, n2 // chunk, freq_chunk, 0)


def _filter_embedding(seq_len):
    n = 2 * seq_len
    pos = np.concatenate([np.arange(seq_len), [0], np.arange(seq_len - 1, 0, -1)])
    t = np.linspace(0.0, 1.0, seq_len, dtype=np.float32)[pos][:, None]
    w = ((2.0 * math.pi / seq_len) * np.arange(seq_len, dtype=np.float32))[pos][:, None]
    bands = np.linspace(1e-4, HY_BANDS - 1, HY_BANDS, dtype=np.float32)[None, :]
    z = np.concatenate([t, np.cos(bands * w), -np.sin(bands * w)], axis=-1).astype(np.float32)
    emb = np.zeros((n, LANES), np.float32)
    emb[:, :HY_EMB] = z
    return jnp.asarray(emb)


def _hyena_filter_spectrum(seq_len, consts, f_w1, f_b1, f_freq, f_w2, f_b2, f_w3, f_b3):
    n = 2 * seq_len
    n2 = consts["n2"]
    ncb = HY_W // HY_CB
    emb = _filter_embedding(seq_len)
    w1p = jnp.zeros((LANES, HY_FO), F32).at[:HY_EMB].set(f_w1)
    deltas = jnp.asarray(np.linspace(HY_MIN_DECAY, HY_MAX_DECAY, HY_W, dtype=np.float32))[None, :]
    const = lambda j: (0, 0)
    out_sds = jax.ShapeDtypeStruct((ncb, n2, FFT_N1 * HY_CB), F32)
    return pl.pallas_call(
        functools.partial(_filter_kernel, seq_len=seq_len, chunk=consts["chunk"]),
        grid=(ncb,),
        in_specs=[
            pl.BlockSpec((n, LANES), const),
            pl.BlockSpec((LANES, HY_FO), const),
            pl.BlockSpec((1, HY_FO), const),
            pl.BlockSpec((2, HY_FO), const),
            pl.BlockSpec((HY_FO, HY_FO), const),
            pl.BlockSpec((1, HY_FO), const),
            pl.BlockSpec((HY_FO, HY_CB), lambda j: (0, j)),
            pl.BlockSpec((HY_FO, HY_CB), lambda j: (0, ncb + j)),
            pl.BlockSpec((1, HY_CB), lambda j: (0, j)),
            pl.BlockSpec((1, HY_CB), lambda j: (0, ncb + j)),
            pl.BlockSpec((1, HY_CB), lambda j: (0, j)),
        ] + [pl.BlockSpec((n2, n2), const, pipeline_mode=pl.Buffered(1))] * 4
          + [pl.BlockSpec((n2, FFT_N1), const, pipeline_mode=pl.Buffered(1))] * 2,
        out_specs=[pl.BlockSpec((1, n2, FFT_N1 * HY_CB), lambda j: (j, 0, 0))] * 2,
        out_shape=[out_sds, out_sds],
        scratch_shapes=[pltpu.VMEM((n, HY_CB), F32), pltpu.VMEM((n, HY_FO), F32)],
        compiler_params=_cparams(("arbitrary",)),
        name=f"hyena_filter_{seq_len}",
    )(emb, w1p, f_b1[None, :], f_freq, f_w2, f_b2[None, :], f_w3, f_w3, f_b3[None, :], f_b3[None, :], deltas,
      *consts["fwd_c"], *consts["fwd_s"], consts["tw_re"], consts["tw_im"])


def _short_conv(p, w_ref, b_ref):
    n = p.shape[0]
    row = lax.broadcasted_iota(jnp.int32, p.shape, 0)
    prev = jnp.where(row == 0, 0.0, pltpu.roll(p, 1, axis=0))
    nxt = jnp.where(row == n - 1, 0.0, pltpu.roll(p, n - 1, axis=0))
    return prev * w_ref[0:1, :] + p * w_ref[1:2, :] + nxt * w_ref[2:3, :] + b_ref[...]


def _hyena_kernel(p0_ref, p1_ref, p2_ref, w0_ref, w1_ref, w2_ref, b0_ref, b1_ref, b2_ref, d_ref,
                  gre_ref, gim_ref, fch_ref, fcl_ref, fsh_ref, fsl_ref, ich_ref, icl_ref, ish_ref, isl_ref,
                  twr_ref, twi_ref, o_ref, z_scr, y_scr, acc_scr, *, seq_len, chunk):
    n2 = 2 * seq_len // FFT_N1
    half = n2 // 2
    x1 = _short_conv(p1_ref[...], w1_ref, b1_ref)
    v = _short_conv(p2_ref[...], w2_ref, b2_ref)
    z_scr[...] = x1 * v
    x = jnp.concatenate([z_scr[pl.ds(s1, half, stride=FFT_N1), :] for s1 in range(FFT_N1)], axis=1)
    xh, xl = _split_bf16(x)
    acc_scr[...] = jnp.zeros(acc_scr.shape, F32)

    def freq_chunk(ci, carry):
        rows = pl.ds(pl.multiple_of(ci * chunk, chunk), chunk)
        zf = _forward_spectrum(xh, xl, rows, fch_ref, fcl_ref, fsh_ref, fsl_ref, twr_ref, twi_ref)
        gre = _lane_blocks(gre_ref[0, rows, :])
        gim = _lane_blocks(gim_ref[0, rows, :])
        prod = [(zr * gr - zi * gi, zr * gi + zi * gr) for (zr, zi), gr, gi in zip(zf, gre, gim)]
        back = _fft_blocks(prod, +1)
        c_re, c_im = [], []
        for s1, (re, im) in enumerate(back):
            tr = twr_ref[rows, s1:s1 + 1]
            ti = -twi_ref[rows, s1:s1 + 1]
            c_re.append(re * tr - im * ti)
            c_im.append(re * ti + im * tr)
        rh, rl = _split_bf16(jnp.concatenate(c_re, axis=1))
        ih, il = _split_bf16(jnp.concatenate(c_im, axis=1))
        acc_scr[...] += _dot3(ich_ref[ci], icl_ref[ci], rh, rl) - _dot3(ish_ref[ci], isl_ref[ci], ih, il)
        return carry

    lax.fori_loop(0, n2 // chunk, freq_chunk, 0)
    for s1 in range(FFT_N1):
        y_scr[pl.ds(s1, half, stride=FFT_N1), :] = acc_scr[:, s1 * LANES:(s1 + 1) * LANES]
    x0 = _short_conv(p0_ref[...], w0_ref, b0_ref)
    o_ref[...] = (x0 * (y_scr[...] + z_scr[...] * d_ref[...])).astype(o_ref.dtype)


def _hyena(hyp, conv_w, conv_b, d_skip, gre, gim, consts, nb, seq_len, row_blk0):
    n2 = consts["n2"]
    chunk = consts["chunk"]
    ncb = HY_W // HY_CB
    half = n2 // 2
    const = lambda j, b: (0, 0)
    once = pl.Buffered(1)
    in_specs = (
        [pl.BlockSpec((seq_len, HY_CB), (lambda j, b, g=g: (row_blk0 + b, g * ncb + j))) for g in range(3)]
        + [pl.BlockSpec((HY_SHORT, HY_CB), (lambda j, b, g=g: (0, g * ncb + j))) for g in range(3)]
        + [pl.BlockSpec((1, HY_CB), (lambda j, b, g=g: (0, g * ncb + j))) for g in range(3)]
        + [pl.BlockSpec((1, HY_CB), lambda j, b: (0, j))]
        + [pl.BlockSpec((1, n2, FFT_N1 * HY_CB), lambda j, b: (j, 0, 0), pipeline_mode=once)] * 2
        + [pl.BlockSpec((n2, half), const, pipeline_mode=once)] * 4
        + [pl.BlockSpec((n2 // chunk, half, chunk), lambda j, b: (0, 0, 0), pipeline_mode=once)] * 4
        + [pl.BlockSpec((n2, FFT_N1), const, pipeline_mode=once)] * 2
    )
    fwd = [a[:, :half] for a in (*consts["fwd_c"], *consts["fwd_s"])]
    args = [hyp, hyp, hyp, conv_w, conv_w, conv_w, conv_b, conv_b, conv_b, d_skip, gre, gim,
            *fwd, *consts["inv_c"], *consts["inv_s"], consts["tw_re"], consts["tw_im"]]
    return pl.pallas_call(
        functools.partial(_hyena_kernel, seq_len=seq_len, chunk=chunk),
        grid=(ncb, nb),
        in_specs=in_specs,
        out_specs=pl.BlockSpec((seq_len, HY_CB), lambda j, b: (b, j)),
        out_shape=jax.ShapeDtypeStruct((nb * seq_len, HY_W), BF16),
        scratch_shapes=[pltpu.VMEM((seq_len, HY_CB), F32), pltpu.VMEM((seq_len, HY_CB), F32),
                        pltpu.VMEM((half, FFT_N1 * HY_CB), F32)],
        compiler_params=_cparams(("arbitrary", "arbitrary")),
        name=f"hyena_mixer_{seq_len}",
    )(*args)


def _pack_pairs(h):
    half = h.shape[1] // 2
    lo = pltpu.bitcast(h[:, :half].astype(BF16).astype(F32), jnp.uint32) >> 16
    hi = pltpu.bitcast(h[:, half:].astype(BF16).astype(F32), jnp.uint32) & jnp.uint32(0xFFFF0000)
    return lo | hi


def _unpack_pairs(u):
    lo = pltpu.bitcast(u << 16, F32)
    hi = pltpu.bitcast(u & jnp.uint32(0xFFFF0000), F32)
    return jnp.concatenate([lo, hi], axis=1).astype(BF16)


def _mix_out_kernel(tmod_ref, x_ref, ya_ref, yh_ref, yn_ref, mod_ref, w_ref, g_ref, b_ref, rwh_ref, rwl_ref, rb_ref,
                    x1_out, h2_out, idx_out, wt_out, *, alpha, n_experts):
    del tmod_ref
    y = _dot(jnp.concatenate([ya_ref[...], yh_ref[...], yn_ref[...]], axis=1), w_ref[...])
    x1 = _layer_norm(alpha * x_ref[...] + mod_ref[0, 2:3, :] * y) * g_ref[...] + b_ref[...]
    x1_out[...] = x1
    h2 = _layer_norm(x1) * (1.0 + mod_ref[0, 4:5, :]) + mod_ref[0, 3:4, :]
    h2_out[...] = _pack_pairs(h2)
    h2_hi, h2_lo = _split_bf16(h2)
    logits = _dot3(h2_hi, h2_lo, rwh_ref[...], rwl_ref[...]) + rb_ref[...]
    lane = lax.broadcasted_iota(jnp.int32, logits.shape, 1)
    lane_f = lane.astype(F32)
    cur = jnp.where(lane < n_experts, logits, -jnp.inf)
    vals, idxs = [], []
    for _ in range(TOP_K):
        m = cur.max(axis=-1, keepdims=True)
        ix = jnp.where(cur == m, lane_f, float(LANES)).min(axis=-1, keepdims=True)
        vals.append(m)
        idxs.append(ix)
        cur = jnp.where(lane_f == ix, -jnp.inf, cur)
    es = [jnp.exp(v - vals[0]) for v in vals]
    denom = es[0]
    for e in es[1:]:
        denom = denom + e
    idx_v = jnp.zeros(logits.shape, F32)
    wt_v = jnp.zeros(logits.shape, F32)
    for k in range(TOP_K):
        idx_v = jnp.where(lane == k, idxs[k], idx_v)
        wt_v = jnp.where(lane == k, es[k] / denom, wt_v)
    idx_out[...] = idx_v.astype(jnp.int32)
    wt_out[...] = wt_v


def _mix_out(xf, y_att, y_hy, y_na, mods_l, w_out_bf, ln_g, ln_b, router_wp, router_bp, tile_mod, n_tok,
             alpha, n_experts):
    d = xf.shape[1]
    tm = TOKEN_TILE
    row = lambda i, tmod: (i, 0)
    const = lambda i, tmod: (0, 0)
    grid_spec = pltpu.PrefetchScalarGridSpec(
        num_scalar_prefetch=1,
        grid=(n_tok // tm,),
        in_specs=[
            pl.BlockSpec((tm, d), row),
            pl.BlockSpec((tm, ATT_W), row),
            pl.BlockSpec((tm, HY_W), row),
            pl.BlockSpec((tm, NA_W), row),
            pl.BlockSpec((1, 6, d), lambda i, tmod: (tmod[i], 0, 0)),
            pl.BlockSpec((D_MIX, d), const),
            pl.BlockSpec((1, d), const),
            pl.BlockSpec((1, d), const),
            pl.BlockSpec((d, LANES), const),
            pl.BlockSpec((d, LANES), const),
            pl.BlockSpec((1, LANES), const),
        ],
        out_specs=[
            pl.BlockSpec((tm, d), row),
            pl.BlockSpec((tm, d // 2), row),
            pl.BlockSpec((tm, LANES), row),
            pl.BlockSpec((tm, LANES), row),
        ],
    )
    return pl.pallas_call(
        functools.partial(_mix_out_kernel, alpha=alpha, n_experts=n_experts),
        grid_spec=grid_spec,
        out_shape=[
            jax.ShapeDtypeStruct((n_tok, d), F32),
            jax.ShapeDtypeStruct((n_tok, d // 2), jnp.uint32),
            jax.ShapeDtypeStruct((n_tok, LANES), jnp.int32),
            jax.ShapeDtypeStruct((n_tok, LANES), F32),
        ],
        compiler_params=_cparams(("arbitrary",)),
        name="mix_out_norm_route",
    )(tile_mod, xf, y_att, y_hy, y_na, mods_l, w_out_bf, ln_g, ln_b, *_split_bf16(router_wp), router_bp)


def _dispatch_kernel(slot_ref, h_ref, xs_in, xs_hbm, sem, *, tm):
    del xs_in

    def row_copy(src_row, dst_row, b):
        return pltpu.make_async_copy(h_ref.at[pl.ds(src_row, 1)], xs_hbm.at[pl.ds(dst_row, 1)], sem.at[b])

    def issue(b):
        def body(t, carry):
            for k in range(TOP_K):
                row_copy(b * tm + t, slot_ref[(b * tm + t) * TOP_K + k], b).start(priority=k % 2)
            return carry

        lax.fori_loop(0, tm, body, 0, unroll=8)

    def drain(b):
        def body(t, carry):
            for k in range(TOP_K):
                row_copy(0, 0, b).wait()
            return carry

        lax.fori_loop(0, tm, body, 0, unroll=8)

    issue(0)
    issue(1)
    drain(0)
    drain(1)


def _dispatch(h2p, slots_flat, n_slots):
    n_tok, hw = h2p.shape
    tm = TOKEN_TILE
    assert n_tok % (2 * tm) == 0
    xs0 = jnp.zeros((n_slots, hw), h2p.dtype)
    return pl.pallas_call(
        functools.partial(_dispatch_kernel, tm=tm),
        grid=(n_tok // (2 * tm),),
        in_specs=[
            pl.BlockSpec((2 * tm * TOP_K,), lambda i: (i,), memory_space=pltpu.SMEM),
            pl.BlockSpec((2 * tm, hw), lambda i: (i, 0)),
            pl.BlockSpec(memory_space=pl.ANY),
        ],
        out_specs=pl.BlockSpec(memory_space=pl.ANY),
        out_shape=jax.ShapeDtypeStruct((n_slots, hw), h2p.dtype),
        scratch_shapes=[pltpu.SemaphoreType.DMA((2,))],
        input_output_aliases={2: 0},
        compiler_params=_cparams(("arbitrary",)),
        name="moe_dispatch",
    )(slots_flat, h2p, xs0)


def _expert_kernel(te_ref, nu_ref, x_ref, w1_ref, b1_ref, w2_ref, b2_ref, o_ref, w1_bf, w2_bf, *, d_expert):
    i = pl.program_id(0)

    @pl.when((i == 0) | (te_ref[i] != te_ref[jnp.maximum(i - 1, 0)]))
    def _():
        w1_bf[...] = w1_ref[0].astype(BF16)
        w2_bf[...] = w2_ref[0].astype(BF16)

    @pl.when(i < nu_ref[0])
    def _():
        x = _unpack_pairs(x_ref[...])
        hid = _dot(x, w1_bf[...]) + b1_ref[0]
        glu = jnp.minimum(hid[:, :d_expert], SWIGLU_LIMIT)
        lin = jnp.clip(hid[:, d_expert:], -SWIGLU_LIMIT, SWIGLU_LIMIT)
        act = glu * jax.nn.sigmoid(SWIGLU_ALPHA * glu) * (lin + 1.0)
        o_ref[...] = _dot(act.astype(BF16), w2_bf[...]) + b2_ref[0]

    @pl.when(i >= nu_ref[0])
    def _():
        o_ref[...] = jnp.zeros(o_ref.shape, o_ref.dtype)


def _experts(xs, tile_expert, n_used, w1, b1, w2, b2, layer):
    n_slots, hw = xs.shape
    n_exp = b1.shape[0]
    _, d, d2 = w1.shape
    d_expert = d2 // 2
    tm = MOE_TILE
    n_tiles = n_slots // tm
    e0 = layer * n_exp
    grid_spec = pltpu.PrefetchScalarGridSpec(
        num_scalar_prefetch=2,
        grid=(n_tiles,),
        in_specs=[
            pl.BlockSpec((tm, hw), lambda i, te, nu: (i, 0)),
            pl.BlockSpec((1, d, d2), lambda i, te, nu: (e0 + te[i], 0, 0)),
            pl.BlockSpec((1, 1, d2), lambda i, te, nu: (te[i], 0, 0)),
            pl.BlockSpec((1, d_expert, d), lambda i, te, nu: (e0 + te[i], 0, 0)),
            pl.BlockSpec((1, 1, d), lambda i, te, nu: (te[i], 0, 0)),
        ],
        out_specs=pl.BlockSpec((tm, d), lambda i, te, nu: (i, 0)),
        scratch_shapes=[pltpu.VMEM((d, d2), BF16), pltpu.VMEM((d_expert, d), BF16)],
    )
    return pl.pallas_call(
        functools.partial(_expert_kernel, d_expert=d_expert),
        grid_spec=grid_spec,
        out_shape=jax.ShapeDtypeStruct((n_slots, d), F32),
        compiler_params=_cparams(("arbitrary",)),
        name="moe_experts",
    )(tile_expert, n_used, xs, w1, b1.reshape(n_exp, 1, d2), w2, b2.reshape(n_exp, 1, d))


def _combine_kernel(tmod_ref, slot_ref, next_ref, x1_ref, wt_ref, mod_ref, g_ref, b_ref, ys_hbm, o_ref, buf, sem,
                    *, tm, alpha):
    del tmod_ref
    i = pl.program_id(0)

    def row_copy(src_row, b, k, t):
        return pltpu.make_async_copy(ys_hbm.at[pl.ds(src_row, 1)], buf.at[b, k, pl.ds(t, 1)], sem.at[b])

    def issue(ref, off, b):
        def body(t, carry):
            for k in range(TOP_K):
                row_copy(ref[off + t * TOP_K + k], b, k, t).start(priority=k % 2)
            return carry

        lax.fori_loop(0, tm, body, 0, unroll=8)

    def drain(b):
        def body(t, carry):
            for k in range(TOP_K):
                row_copy(0, b, k, t).wait()
            return carry

        lax.fori_loop(0, tm, body, 0, unroll=8)

    def finish(b):
        rows = pl.ds(b * tm, tm)
        wt = wt_ref[rows, :]
        moe = buf[b, 0] * wt[:, 0:1]
        for k in range(1, TOP_K):
            moe = moe + buf[b, k] * wt[:, k:k + 1]
        o_ref[rows, :] = _layer_norm(alpha * x1_ref[rows, :] + mod_ref[0, 5:6, :] * moe) * g_ref[...] + b_ref[...]

    @pl.when(i == 0)
    def _():
        issue(slot_ref, 0, 0)

    issue(slot_ref, tm * TOP_K, 1)
    drain(0)
    finish(0)

    @pl.when(i + 1 < pl.num_programs(0))
    def _():
        issue(next_ref, 0, 0)

    drain(1)
    finish(1)


def _combine(x1, wts, slots_flat, ys, mods_l, ln_g, ln_b, tile_mod, alpha):
    n_tok, d = x1.shape
    tm = TOKEN_TILE
    n_steps = n_tok // (2 * tm)
    assert n_tok % (2 * tm) == 0
    row = lambda i, tmod: (i, 0)
    const = lambda i, tmod: (0, 0)
    grid_spec = pltpu.PrefetchScalarGridSpec(
        num_scalar_prefetch=1,
        grid=(n_steps,),
        in_specs=[
            pl.BlockSpec((2 * tm * TOP_K,), lambda i, tmod: (i,), memory_space=pltpu.SMEM),
            pl.BlockSpec((tm * TOP_K,), lambda i, tmod: (jnp.minimum(2 * i + 2, 2 * n_steps - 1),),
                         memory_space=pltpu.SMEM),
            pl.BlockSpec((2 * tm, d), row),
            pl.BlockSpec((2 * tm, LANES), row),
            pl.BlockSpec((1, 6, d), lambda i, tmod: (tmod[2 * i], 0, 0)),
            pl.BlockSpec((1, d), const),
            pl.BlockSpec((1, d), const),
            pl.BlockSpec(memory_space=pl.ANY),
        ],
        out_specs=pl.BlockSpec((2 * tm, d), row),
        scratch_shapes=[pltpu.VMEM((2, TOP_K, tm, d), F32), pltpu.SemaphoreType.DMA((2,))],
    )
    return pl.pallas_call(
        functools.partial(_combine_kernel, tm=tm, alpha=alpha),
        grid_spec=grid_spec,
        out_shape=jax.ShapeDtypeStruct((n_tok, d), F32),
        compiler_params=_cparams(("arbitrary",)),
        name="moe_combine_norm",
    )(tile_mod, slots_flat, slots_flat, x1, wts, mods_l, ln_g, ln_b, ys)


def _routing_tables(idx4, n_experts, tile):
    n_tok = idx4.shape[0]
    blk = TOKEN_TILE
    hit = idx4[:, :, None] == jnp.arange(n_experts, dtype=jnp.int32)[None, None, :]
    onehot = hit.any(axis=1)
    blocks = onehot.reshape(n_tok // blk, blk, n_experts).astype(BF16)
    tri = jnp.asarray(np.tril(np.ones((blk, blk), np.float32), -1), BF16)
    within = jnp.einsum("ij,bjk->bik", tri, blocks, preferred_element_type=F32).astype(jnp.int32)
    block_tot = onehot.reshape(n_tok // blk, blk, n_experts).sum(axis=1, dtype=jnp.int32)
    block_pre = jnp.cumsum(block_tot, axis=0) - block_tot
    pos = (within + block_pre[:, None, :]).reshape(n_tok, n_experts)
    counts = block_tot.sum(axis=0)
    padded = ((counts + tile - 1) // tile) * tile
    ends = jnp.cumsum(padded)
    offs = ends - padded
    slots = jnp.sum(jnp.where(hit, (offs[None, :] + pos)[:, None, :], 0), axis=2)
    n_tiles = (n_tok * TOP_K) // tile + n_experts
    tile_start = jnp.arange(n_tiles, dtype=jnp.int32) * tile
    tile_expert = jnp.minimum(jnp.sum(tile_start[:, None] >= ends[None, :], axis=1), n_experts - 1).astype(jnp.int32)
    n_used = (ends[-1] // tile).astype(jnp.int32)
    last = tile_expert[jnp.maximum(n_used - 1, 0)]
    tile_expert = jnp.where(jnp.arange(n_tiles) < n_used, tile_expert, last)
    return slots.astype(jnp.int32).reshape(-1), tile_expert, n_used.reshape(1), n_tiles * tile


def _rope_tables(n, grid_w, extra_rows):
    t = np.arange(n)
    row = (t // grid_w).astype(np.float32)[:, None]
    col = (t % grid_w).astype(np.float32)[:, None]
    axis_dim = HEAD_DIM // 2
    inv_freq = (ROPE_THETA ** (-np.arange(0, axis_dim, 2, dtype=np.float32) / axis_dim)).astype(np.float32)
    ang_r = row * inv_freq
    ang_c = col * inv_freq
    ang = np.concatenate([ang_r, ang_r, ang_c, ang_c], axis=-1)
    cos = np.cos(ang).astype(np.float32)
    sin = np.sin(ang).astype(np.float32)
    sign = np.where((np.arange(HEAD_DIM) % 32) < 16, -1.0, 1.0).astype(np.float32)
    sin = sin * sign[None, :]
    cos = np.concatenate([cos, np.ones((extra_rows, HEAD_DIM), np.float32)], axis=0)
    sin = np.concatenate([sin, np.zeros((extra_rows, HEAD_DIM), np.float32)], axis=0)
    reps = LANES // HEAD_DIM
    return jnp.asarray(np.tile(cos, (1, reps))), jnp.asarray(np.tile(sin, (1, reps)))


def _forward(x, c, ctx, c_ctx, ada_w, ada_b, w_in, w_out, q_gain, k_gain, hy_conv_w, hy_conv_b,
             hy_w1, hy_b1, hy_freq, hy_w2, hy_b2, hy_w3, hy_b3, hy_d, na_rpb, ln1_g, ln1_b,
             ln2_g, ln2_b, router_w, router_b, exp_w1, exp_b1, exp_w2, exp_b2, *, grid_w):
    nb, n, d = x.shape
    m = ctx.shape[1]
    depth = ada_w.shape[0]
    n_experts = router_w.shape[-1]
    alpha = (2.0 * depth) ** 0.25
    tm = TOKEN_TILE
    assert n % tm == 0 and m % tm == 0 and n % m == 0 and n_experts <= LANES
    assert (n // tm) % 2 == 0 and (nb * m // tm) % 2 == 0
    t_lat, t_all = nb * n, nb * (n + m)

    xf = jnp.concatenate([x.reshape(nb * n, d), ctx.reshape(nb * m, d)], axis=0)
    cvec = jnp.zeros((SUBLANES, d), F32).at[:nb].set(c).at[nb].set(c_ctx)
    mods = _adaln(cvec, ada_w, ada_b).reshape(depth, SUBLANES, 6, d)

    tiles = np.arange(t_all // tm)
    lat_tile = tiles < t_lat // tm
    tile_mod = jnp.asarray(np.where(lat_tile, tiles // (n // tm), nb), jnp.int32)
    tile_pos = jnp.asarray(np.where(lat_tile, tiles % (n // tm), n // tm), jnp.int32)
    cos_t, sin_t = _rope_tables(n, grid_w, tm)
    gmat = jnp.asarray(np.kron(np.eye(ATT_Q_HEADS), np.ones((HEAD_DIM, HEAD_DIM))), BF16)
    dft_lat = _dft_constants(n)
    dft_ctx = _dft_constants(m)

    for l in range(depth):
        need_ctx = l < depth - 1
        n_tok = t_all if need_ctx else t_lat
        q, k, v, hyp, nq, nk, nv = _proj(
            xf, mods[l], w_in[l].astype(BF16), jnp.tile(q_gain[l], ATT_Q_HEADS)[None, :],
            jnp.tile(k_gain[l], ATT_KV_HEADS)[None, :], gmat, cos_t, sin_t, tile_mod, tile_pos)
        y_att = _gqa(q, k, v, nb, n, m, need_ctx)
        filt = (hy_w1[l], hy_b1[l], hy_freq[l], hy_w2[l], hy_b2[l], hy_w3[l], hy_b3[l])
        gre, gim = _hyena_filter_spectrum(n, dft_lat, *filt)
        cb = hy_conv_b[l][None, :]
        dsk = hy_d[l][None, :]
        y_hy = _hyena(hyp, hy_conv_w[l], cb, dsk, gre, gim, dft_lat, nb, n, 0)
        if need_ctx:
            gre_c, gim_c = _hyena_filter_spectrum(m, dft_ctx, *filt)
            y_hy_c = _hyena(hyp, hy_conv_w[l], cb, dsk, gre_c, gim_c, dft_ctx, nb, m, t_lat // m)
            y_hy = jnp.concatenate([y_hy, y_hy_c], axis=0)
        bias_tab = _na_bias_table(na_rpb[l], grid_w, n // grid_w)
        y_na = _na(nq, nk, nv, bias_tab, nb, n, m, grid_w, need_ctx)

        router_wp = jnp.zeros((d, LANES), F32).at[:, :n_experts].set(router_w[l])
        router_bp = jnp.zeros((1, LANES), F32).at[0, :n_experts].set(router_b[l])
        x1, h2p, idx, wts = _mix_out(xf, y_att, y_hy, y_na, mods[l], w_out[l].astype(BF16), ln1_g[l][None, :],
                                     ln1_b[l][None, :], router_wp, router_bp, tile_mod, n_tok, alpha, n_experts)
        slots, tile_expert, n_used, n_slots = _routing_tables(idx[:, :TOP_K], n_experts, MOE_TILE)
        xs = _dispatch(h2p, slots, n_slots)
        ys = _experts(xs, tile_expert, n_used, exp_w1.reshape((-1,) + exp_w1.shape[2:]), exp_b1[l],
                      exp_w2.reshape((-1,) + exp_w2.shape[2:]), exp_b2[l], l)
        xf = _combine(x1, wts, slots, ys, mods[l], ln2_g[l][None, :], ln2_b[l][None, :], tile_mod, alpha)
    return xf[:t_lat].reshape(nb, n, d)


def kernel(x, c, ctx, c_ctx, ada_w, ada_b, w_in, w_out, q_gain, k_gain, hy_conv_w, hy_conv_b, hy_w1, hy_b1,
           hy_freq, hy_w2, hy_b2, hy_w3, hy_b3, hy_d, na_rpb, ln1_g, ln1_b, ln2_g, ln2_b, router_w, router_b,
           exp_w1, exp_b1, exp_w2, exp_b2):
    return _forward(x, c, ctx, c_ctx, ada_w, ada_b, w_in, w_out, q_gain, k_gain, hy_conv_w, hy_conv_b, hy_w1,
                    hy_b1, hy_freq, hy_w2, hy_b2, hy_w3, hy_b3, hy_d, na_rpb, ln1_g, ln1_b, ln2_g, ln2_b,
                    router_w, router_b, exp_w1, exp_b1, exp_w2, exp_b2, grid_w=64)
```

```python
import cmath
import functools
import math

import jax
import jax.numpy as jnp
import numpy as np
from jax import lax
from jax.experimental import pallas as pl
from jax.experimental.pallas import tpu as pltpu

F32 = jnp.float32
BF16 = jnp.bfloat16
HIGHEST = lax.Precision.HIGHEST

HEAD_DIM = 64
ATT_Q_HEADS = 8
ATT_KV_HEADS = 2
ATT_GROUP = ATT_Q_HEADS // ATT_KV_HEADS
ATT_W = ATT_Q_HEADS * HEAD_DIM
KV_W = ATT_KV_HEADS * HEAD_DIM
HY_W = 256
HY_SHORT = 3
HY_EMB = 33
HY_BANDS = (HY_EMB - 1) // 2
HY_FO = 64
HY_MIN_DECAY = math.log(1e-2) / 1.5
HY_MAX_DECAY = math.log(1e-2) / 0.3
NA_HEADS = 4
NA_W = NA_HEADS * HEAD_DIM
NA_WIN_ROWS = 8
NA_WIN_COLS = 16
D_MIX = ATT_W + HY_W + NA_W
D_IN = ATT_W + 2 * KV_W + 3 * HY_W + 3 * NA_W
TOP_K = 4
SWIGLU_ALPHA = 1.702
SWIGLU_LIMIT = 7.0
ROPE_THETA = 10000.0
NORM_EPS = 1e-6
LOG2E = 1.0 / math.log(2.0)
Q_SCALE = HEAD_DIM ** -0.5 * LOG2E

LANES = 128
SUBLANES = 8
NEG_BIG = -1e30

TOKEN_TILE = 512
ROUTE_TILE = 256
ATT_Q_TILE = 256
NA_Q_ROWS = 8
NA_K_ROWS = 16
FFT_N1 = 8
HY_CB = 128
FREQ_CHUNK = 256
MOE_TILE = 512
VMEM_LIMIT = 56 * 1024 * 1024


def _cparams(sem, vmem=VMEM_LIMIT):
    return pltpu.CompilerParams(dimension_semantics=sem, vmem_limit_bytes=vmem)


def _layer_norm(x):
    mu = jnp.mean(x, axis=-1, keepdims=True)
    xc = x - mu
    var = jnp.mean(xc * xc, axis=-1, keepdims=True)
    return xc * lax.rsqrt(var + NORM_EPS)


def _dot(a, b):
    return jnp.dot(a, b, preferred_element_type=F32)


def _dot_nt(a, b):
    return lax.dot_general(a, b, (((1,), (1,)), ((), ())), preferred_element_type=F32)


def _split_bf16(x):
    hi = x.astype(BF16)
    lo = (x - hi.astype(F32)).astype(BF16)
    return hi, lo


def _dot3(a_hi, a_lo, b_hi, b_lo):
    return _dot(a_hi, b_hi) + _dot(a_hi, b_lo) + _dot(a_lo, b_hi)


def _adaln_kernel(c_ref, w_ref, b_ref, o_ref):
    c = c_ref[...]
    a = c * jax.nn.sigmoid(c)
    o_ref[0] = jnp.dot(a, w_ref[0], preferred_element_type=F32, precision=HIGHEST) + b_ref[0]


def _adaln(cvec, ada_w, ada_b):
    n_layers, d, d6 = ada_w.shape
    bn = d6 // 4 if (d6 // 4) % LANES == 0 else d6
    return pl.pallas_call(
        _adaln_kernel,
        grid=(n_layers, d6 // bn),
        in_specs=[
            pl.BlockSpec((SUBLANES, d), lambda l, j: (0, 0)),
            pl.BlockSpec((1, d, bn), lambda l, j: (l, 0, j)),
            pl.BlockSpec((1, 1, bn), lambda l, j: (l, 0, j)),
        ],
        out_specs=pl.BlockSpec((1, SUBLANES, bn), lambda l, j: (l, 0, j)),
        out_shape=jax.ShapeDtypeStruct((n_layers, SUBLANES, d6), F32),
        compiler_params=_cparams(("arbitrary", "arbitrary")),
        name="adaln",
    )(cvec, ada_w, ada_b.reshape(n_layers, 1, d6))


def _head_norm(a, g_ref, gain):
    w = a.shape[1]
    sq = a * a
    hi, lo = _split_bf16(sq)
    g = g_ref[0:w, 0:w]
    ss = _dot(hi, g) + _dot(lo, g)
    return a * lax.rsqrt(ss * (1.0 / HEAD_DIM) + NORM_EPS) * gain


def _rope(a, cos, sin_signed):
    w = a.shape[1]
    reps = w // LANES
    c = jnp.concatenate([cos] * reps, axis=1) if reps > 1 else cos
    s = jnp.concatenate([sin_signed] * reps, axis=1) if reps > 1 else sin_signed
    lane = lax.broadcasted_iota(jnp.int32, a.shape, 1)
    first = (lane & 31) < 16
    rot = jnp.where(first, pltpu.roll(a, w - 16, axis=1), pltpu.roll(a, 16, axis=1))
    return a * c + rot * s


def _proj_kernel(tmod_ref, tpos_ref, x_ref, mod_ref, w_ref, qg_ref, kg_ref, g_ref, cos_ref, sin_ref,
                 q_out, k_out, v_out, hy_out, nq_out, nk_out, nv_out):
    del tmod_ref, tpos_ref
    h = _layer_norm(x_ref[...]) * (1.0 + mod_ref[0, 1:2, :]) + mod_ref[0, 0:1, :]
    p = _dot(h.astype(BF16), w_ref[...])
    o = 0
    aq = p[:, o:o + ATT_W]; o += ATT_W
    ak = p[:, o:o + KV_W]; o += KV_W
    av = p[:, o:o + KV_W]; o += KV_W
    hy = p[:, o:o + 3 * HY_W]; o += 3 * HY_W
    nq = p[:, o:o + NA_W]; o += NA_W
    nk = p[:, o:o + NA_W]; o += NA_W
    nv = p[:, o:o + NA_W]
    cos = cos_ref[...]
    sin = sin_ref[...]
    q = _rope(_head_norm(aq, g_ref, qg_ref[...]), cos, sin) * Q_SCALE
    k = _rope(_head_norm(ak, g_ref, kg_ref[...]), cos, sin)
    q_out[...] = q.astype(BF16)
    k_out[...] = k.astype(BF16)
    v_out[...] = av.astype(BF16)
    hy_out[...] = hy
    nq_out[...] = (nq * Q_SCALE).astype(BF16)
    nk_out[...] = nk.astype(BF16)
    nv_out[...] = nv.astype(BF16)


def _proj(xf, mods_l, w_in_bf, q_gain, k_gain, gmat, cos_t, sin_t, tile_mod, tile_pos):
    t, d = xf.shape
    tm = TOKEN_TILE
    n_tiles = t // tm
    row = lambda i, tmod, tpos: (i, 0)
    const = lambda i, tmod, tpos: (0, 0)
    widths = (ATT_W, KV_W, KV_W, 3 * HY_W, NA_W, NA_W, NA_W)
    dtypes = (BF16, BF16, BF16, F32, BF16, BF16, BF16)
    grid_spec = pltpu.PrefetchScalarGridSpec(
        num_scalar_prefetch=2,
        grid=(n_tiles,),
        in_specs=[
            pl.BlockSpec((tm, d), row),
            pl.BlockSpec((1, 6, d), lambda i, tmod, tpos: (tmod[i], 0, 0)),
            pl.BlockSpec((d, D_IN), const),
            pl.BlockSpec((1, ATT_W), const),
            pl.BlockSpec((1, KV_W), const),
            pl.BlockSpec((ATT_W, ATT_W), const),
            pl.BlockSpec((tm, LANES), lambda i, tmod, tpos: (tpos[i], 0)),
            pl.BlockSpec((tm, LANES), lambda i, tmod, tpos: (tpos[i], 0)),
        ],
        out_specs=[pl.BlockSpec((tm, w), row) for w in widths],
    )
    return pl.pallas_call(
        _proj_kernel,
        grid_spec=grid_spec,
        out_shape=[jax.ShapeDtypeStruct((t, w), dt) for w, dt in zip(widths, dtypes)],
        compiler_params=_cparams(("arbitrary",)),
        name="ln_mod_proj",
    )(tile_mod, tile_pos, xf, mods_l, w_in_bf, q_gain, k_gain, gmat, cos_t, sin_t)


def _attend(q, segs, n_kv, grp, bias=None):
    tq = q.shape[0]
    hd = HEAD_DIM
    outs = []
    for j in range(n_kv):
        heads = [q[:, (j * grp + g) * hd:(j * grp + g + 1) * hd] for g in range(grp)]
        qs = jnp.concatenate(heads, axis=0) if grp > 1 else heads[0]
        scores = []
        for si, (k, _) in enumerate(segs):
            s = _dot_nt(qs, k[:, j * hd:(j + 1) * hd])
            if bias is not None and bias[j][si] is not None:
                s = s + bias[j][si]
            scores.append(s)
        m = scores[0].max(axis=-1, keepdims=True)
        for s in scores[1:]:
            m = jnp.maximum(m, s.max(axis=-1, keepdims=True))
        denom = None
        acc = None
        for s, (_, v) in zip(scores, segs):
            p = jnp.exp2(s - m)
            ps = p.sum(axis=-1, keepdims=True)
            pv = _dot(p.astype(BF16), v[:, j * hd:(j + 1) * hd])
            denom = ps if denom is None else denom + ps
            acc = pv if acc is None else acc + pv
        o = acc / denom
        for g in range(grp):
            outs.append(o[g * tq:(g + 1) * tq])
    return jnp.concatenate(outs, axis=1)


def _gqa_kernel(q_ref, kl_ref, vl_ref, kc_ref, vc_ref, o_ref):
    segs = [(kl_ref[...], vl_ref[...]), (kc_ref[...], vc_ref[...])]
    o_ref[...] = _attend(q_ref[...], segs, ATT_KV_HEADS, ATT_GROUP).astype(o_ref.dtype)


def _gqa(q, k, v, nb, n, m, with_ctx):
    t = q.shape[0]
    tq = ATT_Q_TILE
    n_lat = n // tq
    ctx_blk0 = nb * n // m
    qmap = lambda b, i: (b * n_lat + i, 0)
    lat = lambda b, i: (b, 0)
    ctx = lambda b, i: (ctx_blk0 + b, 0)
    y = pl.pallas_call(
        _gqa_kernel,
        grid=(nb, n_lat),
        in_specs=[
            pl.BlockSpec((tq, ATT_W), qmap),
            pl.BlockSpec((n, KV_W), lat),
            pl.BlockSpec((n, KV_W), lat),
            pl.BlockSpec((m, KV_W), ctx),
            pl.BlockSpec((m, KV_W), ctx),
        ],
        out_specs=pl.BlockSpec((tq, ATT_W), qmap),
        out_shape=jax.ShapeDtypeStruct((nb * n, ATT_W), BF16),
        compiler_params=_cparams(("arbitrary", "arbitrary")),
        name="gqa_attention",
    )(q, k, v, k, v)
    if not with_ctx:
        return y
    yc = _ctx_attention(q, k, v, nb, n, m, ATT_KV_HEADS, ATT_GROUP, ATT_W, "gqa_ctx_attention")
    return jnp.concatenate([y, yc], axis=0)


def _na_kernel(q_ref, kl_ref, vl_ref, kc_ref, vc_ref, bias_ref, o_ref, *, grid_w, grid_rows):
    i = pl.program_id(1)
    krow0 = jnp.clip(i * NA_Q_ROWS - NA_WIN_ROWS // 2, 0, grid_rows - NA_K_ROWS)
    start = pl.multiple_of(krow0 * grid_w, 4 * grid_w)
    nk = NA_K_ROWS * grid_w
    segs = [(kl_ref[pl.ds(start, nk), :], vl_ref[pl.ds(start, nk), :]), (kc_ref[...], vc_ref[...])]
    bias = [[bias_ref[0, h], None] for h in range(NA_HEADS)]
    o_ref[...] = _attend(q_ref[...], segs, NA_HEADS, 1, bias).astype(o_ref.dtype)


def _na_bias_table(rpb, grid_w, grid_rows):
    qr, kr = NA_Q_ROWS, NA_K_ROWS
    n_rr, n_rc = 2 * NA_WIN_ROWS - 1, 2 * NA_WIN_COLS - 1
    c = np.arange(grid_w)[:, None]
    kcol = np.arange(grid_w)[None, :]
    cs = np.clip(c - NA_WIN_COLS // 2, 0, grid_w - NA_WIN_COLS)
    col_ok = (kcol >= cs) & (kcol < cs + NA_WIN_COLS)
    csel = ((kcol - c + NA_WIN_COLS - 1)[:, :, None] == np.arange(n_rc)) & col_ok[:, :, None]
    rsel, mask = [], []
    for r0 in (0, qr, grid_rows - qr):
        k0 = int(np.clip(r0 - NA_WIN_ROWS // 2, 0, grid_rows - kr))
        r = r0 + np.arange(qr)[:, None]
        krow = k0 + np.arange(kr)[None, :]
        rs = np.clip(r - NA_WIN_ROWS // 2, 0, grid_rows - NA_WIN_ROWS)
        row_ok = (krow >= rs) & (krow < rs + NA_WIN_ROWS)
        rsel.append(((krow - r + NA_WIN_ROWS - 1)[:, :, None] == np.arange(n_rr)) & row_ok[:, :, None])
        ok = row_ok[:, None, :, None] & col_ok[None, :, None, :]
        mask.append(np.where(ok, 0.0, NEG_BIG))
    rsel = jnp.asarray(np.stack(rsel), F32)
    mask = jnp.asarray(np.stack(mask), F32)
    vals = jnp.einsum("tjia,hab,ckb->thjcik", rsel, rpb, jnp.asarray(csel, F32), precision=HIGHEST)
    return (vals * LOG2E + mask[:, None]).reshape(3, NA_HEADS, qr * grid_w, kr * grid_w)


def _na(nq, nk, nv, bias_tab, nb, n, m, grid_w, with_ctx):
    t = nq.shape[0]
    grid_rows = n // grid_w
    tq = NA_Q_ROWS * grid_w
    n_lat = n // tq
    ctx_blk0 = nb * n // m
    lat = lambda b, i: (b, 0)
    ctx = lambda b, i: (ctx_blk0 + b, 0)
    qmap = lambda b, i: (b * n_lat + i, 0)

    def bmap(b, i):
        return (jnp.where(i == 0, 0, jnp.where(i >= n_lat - 1, 2, 1)), 0, 0, 0)

    y = pl.pallas_call(
        functools.partial(_na_kernel, grid_w=grid_w, grid_rows=grid_rows),
        grid=(nb, n_lat),
        in_specs=[
            pl.BlockSpec((tq, NA_W), qmap),
            pl.BlockSpec((n, NA_W), lat),
            pl.BlockSpec((n, NA_W), lat),
            pl.BlockSpec((m, NA_W), ctx),
            pl.BlockSpec((m, NA_W), ctx),
            pl.BlockSpec((1, NA_HEADS, tq, NA_K_ROWS * grid_w), bmap),
        ],
        out_specs=pl.BlockSpec((tq, NA_W), qmap),
        out_shape=jax.ShapeDtypeStruct((nb * n, NA_W), BF16),
        compiler_params=_cparams(("arbitrary", "arbitrary")),
        name="neighbourhood_attention",
    )(nq, nk, nv, nk, nv, bias_tab)
    if not with_ctx:
        return y
    yc = _ctx_attention(nq, nk, nv, nb, n, m, NA_HEADS, 1, NA_W, "na_ctx_attention")
    return jnp.concatenate([y, yc], axis=0)


def _ctx_attn_kernel(q_ref, k_ref, v_ref, o_ref, *, n_kv, grp):
    segs = [(k_ref[...], v_ref[...])]
    o_ref[...] = _attend(q_ref[...], segs, n_kv, grp).astype(o_ref.dtype)


def _ctx_attention(q, k, v, nb, n, m, n_kv, grp, width, name):
    ctx_blk0 = nb * n // m
    ctx = lambda b: (ctx_blk0 + b, 0)
    kw = n_kv * HEAD_DIM
    return pl.pallas_call(
        functools.partial(_ctx_attn_kernel, n_kv=n_kv, grp=grp),
        grid=(nb,),
        in_specs=[
            pl.BlockSpec((m, width), ctx),
            pl.BlockSpec((m, kw), ctx),
            pl.BlockSpec((m, kw), ctx),
        ],
        out_specs=pl.BlockSpec((m, width), lambda b: (b, 0)),
        out_shape=jax.ShapeDtypeStruct((nb * m, width), BF16),
        compiler_params=_cparams(("arbitrary",)),
        name=name,
    )(q, k, v)


def _dft_constants(seq_len):
    n = 2 * seq_len
    n2 = n // FFT_N1
    idx = np.arange(n2, dtype=np.int64)
    ang = 2.0 * np.pi * ((idx[:, None] * idx[None, :]) % n2).astype(np.float64) / n2
    cmat, smat = np.cos(ang), np.sin(ang)

    def split(a):
        hi = jnp.asarray(a, F32).astype(BF16)
        lo = (jnp.asarray(a, F32) - hi.astype(F32)).astype(BF16)
        return hi, lo

    def col_chunks(a):
        return np.transpose(a.reshape(a.shape[0], n2 // chunk, chunk), (1, 0, 2))

    chunk = min(FREQ_CHUNK, n2)
    tw_ang = 2.0 * np.pi * (idx[:, None] * np.arange(FFT_N1)[None, :]).astype(np.float64) / n
    return dict(
        n2=n2, chunk=chunk,
        fwd_c=split(cmat), fwd_s=split(smat),
        inv_c=split(col_chunks(cmat[: n2 // 2] / n)),
        inv_s=split(col_chunks(smat[: n2 // 2] / n)),
        tw_re=jnp.asarray(np.cos(tw_ang), F32), tw_im=jnp.asarray(-np.sin(tw_ang), F32),
    )


def _cmul_const(a, w):
    re, im = a
    if abs(w - 1) < 1e-12:
        return a
    if abs(w + 1) < 1e-12:
        return (-re, -im)
    if abs(w + 1j) < 1e-12:
        return (im, -re)
    if abs(w - 1j) < 1e-12:
        return (-im, re)
    return (re * w.real - im * w.imag, re * w.imag + im * w.real)


def _fft_blocks(xs, sign):
    n = len(xs)
    if n == 1:
        return xs
    even = _fft_blocks(xs[0::2], sign)
    odd = _fft_blocks(xs[1::2], sign)
    out = [None] * n
    for k in range(n // 2):
        t = _cmul_const(odd[k], cmath.exp(sign * 2j * cmath.pi * k / n))
        out[k] = (even[k][0] + t[0], even[k][1] + t[1])
        out[k + n // 2] = (even[k][0] - t[0], even[k][1] - t[1])
    return out


def _lane_blocks(a):
    return [a[:, j * LANES:(j + 1) * LANES] for j in range(a.shape[1] // LANES)]


def _forward_spectrum(xh, xl, rows, ch_ref, cl_ref, sh_ref, sl_ref, twr_ref, twi_ref):
    a_re = _dot3(ch_ref[rows, :], cl_ref[rows, :], xh, xl)
    a_im = -_dot3(sh_ref[rows, :], sl_ref[rows, :], xh, xl)
    blocks = []
    for s1, (re, im) in enumerate(zip(_lane_blocks(a_re), _lane_blocks(a_im))):
        tr = twr_ref[rows, s1:s1 + 1]
        ti = twi_ref[rows, s1:s1 + 1]
        blocks.append((re * tr - im * ti, re * ti + im * tr))
    return _fft_blocks(blocks, -1)


def _filter_kernel(emb_ref, w1_ref, b1_ref, fr_ref, w2_ref, b2_ref, w3f_ref, w3b_ref, b3f_ref, b3b_ref,
                   dec_ref, ch_ref, cl_ref, sh_ref, sl_ref, twr_ref, twi_ref, gre_ref, gim_ref, g_scr, hid_scr, *,
                   seq_len, chunk):
    n = 2 * seq_len
    n2 = n // FFT_N1
    def mm(a, w_ref):
        return _dot3(*_split_bf16(a), *_split_bf16(w_ref[...]))

    @pl.when(pl.program_id(0) == 0)
    def _():
        hid1 = jnp.sin(fr_ref[0:1, :] * (mm(emb_ref[...], w1_ref) + b1_ref[...]))
        hid_scr[...] = jnp.sin(fr_ref[1:2, :] * (mm(hid1, w2_ref) + b2_ref[...]))

    hid = hid_scr[...]
    f_fwd = mm(hid, w3f_ref) + b3f_ref[...]
    f_bwd = mm(hid, w3b_ref) + b3b_ref[...]
    row = lax.broadcasted_iota(jnp.int32, f_fwd.shape, 0)
    t_pos = emb_ref[:, 0:1]
    g = jnp.where(row < seq_len, f_fwd, jnp.where(row > seq_len, f_bwd, 0.0))
    g = g * jnp.exp(-t_pos * jnp.abs(dec_ref[...]))
    g = g * lax.rsqrt(jnp.sum(g * g, axis=0, keepdims=True) + NORM_EPS)
    g_scr[...] = g
    x = jnp.concatenate([g_scr[pl.ds(s1, n2, stride=FFT_N1), :] for s1 in range(FFT_N1)], axis=1)
    xh, xl = _split_bf16(x)

    def freq_chunk(ci, carry):
        rows = pl.ds(pl.multiple_of(ci * chunk, chunk), chunk)
        spec = _forward_spectrum(xh, xl, rows, ch_ref, cl_ref, sh_ref, sl_ref, twr_ref, twi_ref)
        gre_ref[0, rows, :] = jnp.concatenate([b[0] for b in spec], axis=1)
        gim_ref[0, rows, :] = jnp.concatenate([b[1] for b in spec], axis=1)
        return carry

    lax.fori_loop(0, n2 // chunk, freq_chunk, 0)


def _filter_embedding(seq_len):
    n = 2 * seq_len
    pos = np.concatenate([np.arange(seq_len), [0], np.arange(seq_len - 1, 0, -1)])
    t = np.linspace(0.0, 1.0, seq_len, dtype=np.float32)[pos][:, None]
    w = ((2.0 * math.pi / seq_len) * np.arange(seq_len, dtype=np.float32))[pos][:, None]
    bands = np.linspace(1e-4, HY_BANDS - 1, HY_BANDS, dtype=np.float32)[None, :]
    z = np.concatenate([t, np.cos(bands * w), -np.sin(bands * w)], axis=-1).astype(np.float32)
    emb = np.zeros((n, LANES), np.float32)
    emb[:, :HY_EMB] = z
    return jnp.asarray(emb)


def _hyena_filter_spectrum(seq_len, consts, f_w1, f_b1, f_freq, f_w2, f_b2, f_w3, f_b3):
    n = 2 * seq_len
    n2 = consts["n2"]
    ncb = HY_W // HY_CB
    emb = _filter_embedding(seq_len)
    w1p = jnp.zeros((LANES, HY_FO), F32).at[:HY_EMB].set(f_w1)
    deltas = jnp.asarray(np.linspace(HY_MIN_DECAY, HY_MAX_DECAY, HY_W, dtype=np.float32))[None, :]
    const = lambda j: (0, 0)
    out_sds = jax.ShapeDtypeStruct((ncb, n2, FFT_N1 * HY_CB), F32)
    return pl.pallas_call(
        functools.partial(_filter_kernel, seq_len=seq_len, chunk=consts["chunk"]),
        grid=(ncb,),
        in_specs=[
            pl.BlockSpec((n, LANES), const),
            pl.BlockSpec((LANES, HY_FO), const),
            pl.BlockSpec((1, HY_FO), const),
            pl.BlockSpec((2, HY_FO), const),
            pl.BlockSpec((HY_FO, HY_FO), const),
            pl.BlockSpec((1, HY_FO), const),
            pl.BlockSpec((HY_FO, HY_CB), lambda j: (0, j)),
            pl.BlockSpec((HY_FO, HY_CB), lambda j: (0, ncb + j)),
            pl.BlockSpec((1, HY_CB), lambda j: (0, j)),
            pl.BlockSpec((1, HY_CB), lambda j: (0, ncb + j)),
            pl.BlockSpec((1, HY_CB), lambda j: (0, j)),
        ] + [pl.BlockSpec((n2, n2), const, pipeline_mode=pl.Buffered(1))] * 4
          + [pl.BlockSpec((n2, FFT_N1), const, pipeline_mode=pl.Buffered(1))] * 2,
        out_specs=[pl.BlockSpec((1, n2, FFT_N1 * HY_CB), lambda j: (j, 0, 0))] * 2,
        out_shape=[out_sds, out_sds],
        scratch_shapes=[pltpu.VMEM((n, HY_CB), F32), pltpu.VMEM((n, HY_FO), F32)],
        compiler_params=_cparams(("arbitrary",)),
        name=f"hyena_filter_{seq_len}",
    )(emb, w1p, f_b1[None, :], f_freq, f_w2, f_b2[None, :], f_w3, f_w3, f_b3[None, :], f_b3[None, :], deltas,
      *consts["fwd_c"], *consts["fwd_s"], consts["tw_re"], consts["tw_im"])


def _short_conv(p, w_ref, b_ref):
    n = p.shape[0]
    row = lax.broadcasted_iota(jnp.int32, p.shape, 0)
    prev = jnp.where(row == 0, 0.0, pltpu.roll(p, 1, axis=0))
    nxt = jnp.where(row == n - 1, 0.0, pltpu.roll(p, n - 1, axis=0))
    return prev * w_ref[0:1, :] + p * w_ref[1:2, :] + nxt * w_ref[2:3, :] + b_ref[...]


def _hyena_kernel(p0_ref, p1_ref, p2_ref, w0_ref, w1_ref, w2_ref, b0_ref, b1_ref, b2_ref, d_ref,
                  gre_ref, gim_ref, fch_ref, fcl_ref, fsh_ref, fsl_ref, ich_ref, icl_ref, ish_ref, isl_ref,
                  twr_ref, twi_ref, o_ref, z_scr, y_scr, acc_scr, *, seq_len, chunk):
    n2 = 2 * seq_len // FFT_N1
    half = n2 // 2
    x1 = _short_conv(p1_ref[...], w1_ref, b1_ref)
    v = _short_conv(p2_ref[...], w2_ref, b2_ref)
    z_scr[...] = x1 * v
    x = jnp.concatenate([z_scr[pl.ds(s1, half, stride=FFT_N1), :] for s1 in range(FFT_N1)], axis=1)
    xh, xl = _split_bf16(x)
    acc_scr[...] = jnp.zeros(acc_scr.shape, F32)

    def freq_chunk(ci, carry):
        rows = pl.ds(pl.multiple_of(ci * chunk, chunk), chunk)
        zf = _forward_spectrum(xh, xl, rows, fch_ref, fcl_ref, fsh_ref, fsl_ref, twr_ref, twi_ref)
        gre = _lane_blocks(gre_ref[0, rows, :])
        gim = _lane_blocks(gim_ref[0, rows, :])
        prod = [(zr * gr - zi * gi, zr * gi + zi * gr) for (zr, zi), gr, gi in zip(zf, gre, gim)]
        back = _fft_blocks(prod, +1)
        c_re, c_im = [], []
        for s1, (re, im) in enumerate(back):
            tr = twr_ref[rows, s1:s1 + 1]
            ti = -twi_ref[rows, s1:s1 + 1]
            c_re.append(re * tr - im * ti)
            c_im.append(re * ti + im * tr)
        rh, rl = _split_bf16(jnp.concatenate(c_re, axis=1))
        ih, il = _split_bf16(jnp.concatenate(c_im, axis=1))
        acc_scr[...] += _dot3(ich_ref[ci], icl_ref[ci], rh, rl) - _dot3(ish_ref[ci], isl_ref[ci], ih, il)
        return carry

    lax.fori_loop(0, n2 // chunk, freq_chunk, 0)
    for s1 in range(FFT_N1):
        y_scr[pl.ds(s1, half, stride=FFT_N1), :] = acc_scr[:, s1 * LANES:(s1 + 1) * LANES]
    x0 = _short_conv(p0_ref[...], w0_ref, b0_ref)
    o_ref[...] = (x0 * (y_scr[...] + z_scr[...] * d_ref[...])).astype(o_ref.dtype)


def _hyena(hyp, conv_w, conv_b, d_skip, gre, gim, consts, nb, seq_len, row_blk0):
    n2 = consts["n2"]
    chunk = consts["chunk"]
    ncb = HY_W // HY_CB
    half = n2 // 2
    const = lambda j, b: (0, 0)
    once = pl.Buffered(1)
    in_specs = (
        [pl.BlockSpec((seq_len, HY_CB), (lambda j, b, g=g: (row_blk0 + b, g * ncb + j))) for g in range(3)]
        + [pl.BlockSpec((HY_SHORT, HY_CB), (lambda j, b, g=g: (0, g * ncb + j))) for g in range(3)]
        + [pl.BlockSpec((1, HY_CB), (lambda j, b, g=g: (0, g * ncb + j))) for g in range(3)]
        + [pl.BlockSpec((1, HY_CB), lambda j, b: (0, j))]
        + [pl.BlockSpec((1, n2, FFT_N1 * HY_CB), lambda j, b: (j, 0, 0), pipeline_mode=once)] * 2
        + [pl.BlockSpec((n2, half), const, pipeline_mode=once)] * 4
        + [pl.BlockSpec((n2 // chunk, half, chunk), lambda j, b: (0, 0, 0), pipeline_mode=once)] * 4
        + [pl.BlockSpec((n2, FFT_N1), const, pipeline_mode=once)] * 2
    )
    fwd = [a[:, :half] for a in (*consts["fwd_c"], *consts["fwd_s"])]
    args = [hyp, hyp, hyp, conv_w, conv_w, conv_w, conv_b, conv_b, conv_b, d_skip, gre, gim,
            *fwd, *consts["inv_c"], *consts["inv_s"], consts["tw_re"], consts["tw_im"]]
    return pl.pallas_call(
        functools.partial(_hyena_kernel, seq_len=seq_len, chunk=chunk),
        grid=(ncb, nb),
        in_specs=in_specs,
        out_specs=pl.BlockSpec((seq_len, HY_CB), lambda j, b: (b, j)),
        out_shape=jax.ShapeDtypeStruct((nb * seq_len, HY_W), BF16),
        scratch_shapes=[pltpu.VMEM((seq_len, HY_CB), F32), pltpu.VMEM((seq_len, HY_CB), F32),
                        pltpu.VMEM((half, FFT_N1 * HY_CB), F32)],
        compiler_params=_cparams(("arbitrary", "arbitrary")),
        name=f"hyena_mixer_{seq_len}",
    )(*args)


def _pack_pairs(h):
    half = h.shape[1] // 2
    lo = pltpu.bitcast(h[:, :half].astype(BF16).astype(F32), jnp.uint32) >> 16
    hi = pltpu.bitcast(h[:, half:].astype(BF16).astype(F32), jnp.uint32) & jnp.uint32(0xFFFF0000)
    return lo | hi


def _unpack_pairs(u):
    lo = pltpu.bitcast(u << 16, F32)
    hi = pltpu.bitcast(u & jnp.uint32(0xFFFF0000), F32)
    return jnp.concatenate([lo, hi], axis=1).astype(BF16)


def _mix_out_kernel(tmod_ref, x_ref, ya_ref, yh_ref, yn_ref, mod_ref, w_ref, g_ref, b_ref, rwh_ref, rwl_ref, rb_ref,
                    x1_out, h2_out, idx_out, wt_out, *, alpha, n_experts):
    del tmod_ref
    y = _dot(jnp.concatenate([ya_ref[...], yh_ref[...], yn_ref[...]], axis=1), w_ref[...])
    x1 = _layer_norm(alpha * x_ref[...] + mod_ref[0, 2:3, :] * y) * g_ref[...] + b_ref[...]
    x1_out[...] = x1
    h2 = _layer_norm(x1) * (1.0 + mod_ref[0, 4:5, :]) + mod_ref[0, 3:4, :]
    h2_out[...] = _pack_pairs(h2)
    h2_hi, h2_lo = _split_bf16(h2)
    logits = _dot3(h2_hi, h2_lo, rwh_ref[...], rwl_ref[...]) + rb_ref[...]
    lane = lax.broadcasted_iota(jnp.int32, logits.shape, 1)
    lane_f = lane.astype(F32)
    cur = jnp.where(lane < n_experts, logits, -jnp.inf)
    vals, idxs = [], []
    for _ in range(TOP_K):
        m = cur.max(axis=-1, keepdims=True)
        ix = jnp.where(cur == m, lane_f, float(LANES)).min(axis=-1, keepdims=True)
        vals.append(m)
        idxs.append(ix)
        cur = jnp.where(lane_f == ix, -jnp.inf, cur)
    es = [jnp.exp(v - vals[0]) for v in vals]
    denom = es[0]
    for e in es[1:]:
        denom = denom + e
    idx_v = jnp.zeros(logits.shape, F32)
    wt_v = jnp.zeros(logits.shape, F32)
    for k in range(TOP_K):
        idx_v = jnp.where(lane == k, idxs[k], idx_v)
        wt_v = jnp.where(lane == k, es[k] / denom, wt_v)
    idx_out[...] = idx_v.astype(jnp.int32)
    wt_out[...] = wt_v


def _mix_out(xf, y_att, y_hy, y_na, mods_l, w_out_bf, ln_g, ln_b, router_wp, router_bp, tile_mod, n_tok,
             alpha, n_experts):
    d = xf.shape[1]
    tm = TOKEN_TILE
    row = lambda i, tmod: (i, 0)
    const = lambda i, tmod: (0, 0)
    grid_spec = pltpu.PrefetchScalarGridSpec(
        num_scalar_prefetch=1,
        grid=(n_tok // tm,),
        in_specs=[
            pl.BlockSpec((tm, d), row),
            pl.BlockSpec((tm, ATT_W), row),
            pl.BlockSpec((tm, HY_W), row),
            pl.BlockSpec((tm, NA_W), row),
            pl.BlockSpec((1, 6, d), lambda i, tmod: (tmod[i], 0, 0)),
            pl.BlockSpec((D_MIX, d), const),
            pl.BlockSpec((1, d), const),
            pl.BlockSpec((1, d), const),
            pl.BlockSpec((d, LANES), const),
            pl.BlockSpec((d, LANES), const),
            pl.BlockSpec((1, LANES), const),
        ],
        out_specs=[
            pl.BlockSpec((tm, d), row),
            pl.BlockSpec((tm, d // 2), row),
            pl.BlockSpec((tm, LANES), row),
            pl.BlockSpec((tm, LANES), row),
        ],
    )
    return pl.pallas_call(
        functools.partial(_mix_out_kernel, alpha=alpha, n_experts=n_experts),
        grid_spec=grid_spec,
        out_shape=[
            jax.ShapeDtypeStruct((n_tok, d), F32),
            jax.ShapeDtypeStruct((n_tok, d // 2), jnp.uint32),
            jax.ShapeDtypeStruct((n_tok, LANES), jnp.int32),
            jax.ShapeDtypeStruct((n_tok, LANES), F32),
        ],
        compiler_params=_cparams(("arbitrary",)),
        name="mix_out_norm_route",
    )(tile_mod, xf, y_att, y_hy, y_na, mods_l, w_out_bf, ln_g, ln_b, *_split_bf16(router_wp), router_bp)


def _dispatch_kernel(slot_ref, h_ref, xs_in, xs_hbm, sem, *, tm):
    del xs_in

    def row_copy(src_row, dst_row, b):
        return pltpu.make_async_copy(h_ref.at[pl.ds(src_row, 1)], xs_hbm.at[pl.ds(dst_row, 1)], sem.at[b])

    def issue(b):
        def body(t, carry):
            for k in range(TOP_K):
                row_copy(b * tm + t, slot_ref[(b * tm + t) * TOP_K + k], b).start(priority=k % 2)
            return carry

        lax.fori_loop(0, tm, body, 0, unroll=8)

    def drain(b):
        def body(t, carry):
            for k in range(TOP_K):
                row_copy(0, 0, b).wait()
            return carry

        lax.fori_loop(0, tm, body, 0, unroll=8)

    issue(0)
    issue(1)
    drain(0)
    drain(1)


def _dispatch(h2p, slots_flat, n_slots):
    n_tok, hw = h2p.shape
    tm = ROUTE_TILE
    assert n_tok % (2 * tm) == 0
    xs0 = jnp.zeros((n_slots, hw), h2p.dtype)
    return pl.pallas_call(
        functools.partial(_dispatch_kernel, tm=tm),
        grid=(n_tok // (2 * tm),),
        in_specs=[
            pl.BlockSpec((2 * tm * TOP_K,), lambda i: (i,), memory_space=pltpu.SMEM),
            pl.BlockSpec((2 * tm, hw), lambda i: (i, 0)),
            pl.BlockSpec(memory_space=pl.ANY),
        ],
        out_specs=pl.BlockSpec(memory_space=pl.ANY),
        out_shape=jax.ShapeDtypeStruct((n_slots, hw), h2p.dtype),
        scratch_shapes=[pltpu.SemaphoreType.DMA((2,))],
        input_output_aliases={2: 0},
        compiler_params=_cparams(("arbitrary",)),
        name="moe_dispatch",
    )(slots_flat, h2p, xs0)


def _expert_kernel(te_ref, nu_ref, x_ref, w1_ref, b1_ref, w2_ref, b2_ref, o_ref, w1_bf, w2_bf, *, d_expert):
    i = pl.program_id(0)

    @pl.when((i == 0) | (te_ref[i] != te_ref[jnp.maximum(i - 1, 0)]))
    def _():
        w1_bf[...] = w1_ref[0].astype(BF16)
        w2_bf[...] = w2_ref[0].astype(BF16)

    @pl.when(i < nu_ref[0])
    def _():
        x = _unpack_pairs(x_ref[...])
        hid = _dot(x, w1_bf[...]) + b1_ref[0]
        glu = jnp.minimum(hid[:, :d_expert], SWIGLU_LIMIT)
        lin = jnp.clip(hid[:, d_expert:], -SWIGLU_LIMIT, SWIGLU_LIMIT)
        act = glu * jax.nn.sigmoid(SWIGLU_ALPHA * glu) * (lin + 1.0)
        o_ref[...] = _dot(act.astype(BF16), w2_bf[...]) + b2_ref[0]

    @pl.when(i >= nu_ref[0])
    def _():
        o_ref[...] = jnp.zeros(o_ref.shape, o_ref.dtype)


def _experts(xs, tile_expert, n_used, w1, b1, w2, b2, layer):
    n_slots, hw = xs.shape
    n_exp = b1.shape[0]
    _, d, d2 = w1.shape
    d_expert = d2 // 2
    tm = MOE_TILE
    n_tiles = n_slots // tm
    e0 = layer * n_exp
    grid_spec = pltpu.PrefetchScalarGridSpec(
        num_scalar_prefetch=2,
        grid=(n_tiles,),
        in_specs=[
            pl.BlockSpec((tm, hw), lambda i, te, nu: (i, 0)),
            pl.BlockSpec((1, d, d2), lambda i, te, nu: (e0 + te[i], 0, 0)),
            pl.BlockSpec((1, 1, d2), lambda i, te, nu: (te[i], 0, 0)),
            pl.BlockSpec((1, d_expert, d), lambda i, te, nu: (e0 + te[i], 0, 0)),
            pl.BlockSpec((1, 1, d), lambda i, te, nu: (te[i], 0, 0)),
        ],
        out_specs=pl.BlockSpec((tm, d), lambda i, te, nu: (i, 0)),
        scratch_shapes=[pltpu.VMEM((d, d2), BF16), pltpu.VMEM((d_expert, d), BF16)],
    )
    return pl.pallas_call(
        functools.partial(_expert_kernel, d_expert=d_expert),
        grid_spec=grid_spec,
        out_shape=jax.ShapeDtypeStruct((n_slots, d), F32),
        compiler_params=_cparams(("arbitrary",)),
        name="moe_experts",
    )(tile_expert, n_used, xs, w1, b1.reshape(n_exp, 1, d2), w2, b2.reshape(n_exp, 1, d))


def _combine_kernel(tmod_ref, slot_ref, next_ref, x1_ref, wt_ref, mod_ref, g_ref, b_ref, ys_hbm, o_ref, buf, sem,
                    *, tm, alpha):
    del tmod_ref
    i = pl.program_id(0)

    def row_copy(src_row, b, k, t):
        return pltpu.make_async_copy(ys_hbm.at[pl.ds(src_row, 1)], buf.at[b, k, pl.ds(t, 1)], sem.at[b])

    def issue(ref, off, b):
        def body(t, carry):
            for k in range(TOP_K):
                row_copy(ref[off + t * TOP_K + k], b, k, t).start(priority=k % 2)
            return carry

        lax.fori_loop(0, tm, body, 0, unroll=8)

    def drain(b):
        def body(t, carry):
            for k in range(TOP_K):
                row_copy(0, b, k, t).wait()
            return carry

        lax.fori_loop(0, tm, body, 0, unroll=8)

    def finish(b):
        rows = pl.ds(b * tm, tm)
        wt = wt_ref[rows, :]
        moe = buf[b, 0] * wt[:, 0:1]
        for k in range(1, TOP_K):
            moe = moe + buf[b, k] * wt[:, k:k + 1]
        o_ref[rows, :] = _layer_norm(alpha * x1_ref[rows, :] + mod_ref[0, 5:6, :] * moe) * g_ref[...] + b_ref[...]

    @pl.when(i == 0)
    def _():
        issue(slot_ref, 0, 0)

    issue(slot_ref, tm * TOP_K, 1)
    drain(0)
    finish(0)

    @pl.when(i + 1 < pl.num_programs(0))
    def _():
        issue(next_ref, 0, 0)

    drain(1)
    finish(1)


def _combine(x1, wts, slots_flat, ys, mods_l, ln_g, ln_b, tile_mod, alpha):
    n_tok, d = x1.shape
    tm = ROUTE_TILE
    n_steps = n_tok // (2 * tm)
    assert n_tok % (2 * tm) == 0
    row = lambda i, tmod: (i, 0)
    const = lambda i, tmod: (0, 0)
    grid_spec = pltpu.PrefetchScalarGridSpec(
        num_scalar_prefetch=1,
        grid=(n_steps,),
        in_specs=[
            pl.BlockSpec((2 * tm * TOP_K,), lambda i, tmod: (i,), memory_space=pltpu.SMEM),
            pl.BlockSpec((tm * TOP_K,), lambda i, tmod: (jnp.minimum(2 * i + 2, 2 * n_steps - 1),),
                         memory_space=pltpu.SMEM),
            pl.BlockSpec((2 * tm, d), row),
            pl.BlockSpec((2 * tm, LANES), row),
            pl.BlockSpec((1, 6, d), lambda i, tmod: (tmod[2 * i], 0, 0)),
            pl.BlockSpec((1, d), const),
            pl.BlockSpec((1, d), const),
            pl.BlockSpec(memory_space=pl.ANY),
        ],
        out_specs=pl.BlockSpec((2 * tm, d), row),
        scratch_shapes=[pltpu.VMEM((2, TOP_K, tm, d), F32), pltpu.SemaphoreType.DMA((2,))],
    )
    return pl.pallas_call(
        functools.partial(_combine_kernel, tm=tm, alpha=alpha),
        grid_spec=grid_spec,
        out_shape=jax.ShapeDtypeStruct((n_tok, d), F32),
        compiler_params=_cparams(("arbitrary",)),
        name="moe_combine_norm",
    )(tile_mod, slots_flat, slots_flat, x1, wts, mods_l, ln_g, ln_b, ys)


def _routing_tables(idx4, n_experts, tile):
    n_tok = idx4.shape[0]
    blk = ROUTE_TILE
    hit = idx4[:, :, None] == jnp.arange(n_experts, dtype=jnp.int32)[None, None, :]
    onehot = hit.any(axis=1)
    blocks = onehot.reshape(n_tok // blk, blk, n_experts).astype(BF16)
    tri = jnp.asarray(np.tril(np.ones((blk, blk), np.float32), -1), BF16)
    within = jnp.einsum("ij,bjk->bik", tri, blocks, preferred_element_type=F32).astype(jnp.int32)
    block_tot = onehot.reshape(n_tok // blk, blk, n_experts).sum(axis=1, dtype=jnp.int32)
    block_pre = jnp.cumsum(block_tot, axis=0) - block_tot
    pos = (within + block_pre[:, None, :]).reshape(n_tok, n_experts)
    counts = block_tot.sum(axis=0)
    padded = ((counts + tile - 1) // tile) * tile
    ends = jnp.cumsum(padded)
    offs = ends - padded
    slots = jnp.sum(jnp.where(hit, (offs[None, :] + pos)[:, None, :], 0), axis=2)
    n_tiles = (n_tok * TOP_K) // tile + n_experts
    tile_start = jnp.arange(n_tiles, dtype=jnp.int32) * tile
    tile_expert = jnp.minimum(jnp.sum(tile_start[:, None] >= ends[None, :], axis=1), n_experts - 1).astype(jnp.int32)
    n_used = (ends[-1] // tile).astype(jnp.int32)
    last = tile_expert[jnp.maximum(n_used - 1, 0)]
    tile_expert = jnp.where(jnp.arange(n_tiles) < n_used, tile_expert, last)
    return slots.astype(jnp.int32).reshape(-1), tile_expert, n_used.reshape(1), n_tiles * tile


def _rope_tables(n, grid_w, extra_rows):
    t = np.arange(n)
    row = (t // grid_w).astype(np.float32)[:, None]
    col = (t % grid_w).astype(np.float32)[:, None]
    axis_dim = HEAD_DIM // 2
    inv_freq = (ROPE_THETA ** (-np.arange(0, axis_dim, 2, dtype=np.float32) / axis_dim)).astype(np.float32)
    ang_r = row * inv_freq
    ang_c = col * inv_freq
    ang = np.concatenate([ang_r, ang_r, ang_c, ang_c], axis=-1)
    cos = np.cos(ang).astype(np.float32)
    sin = np.sin(ang).astype(np.float32)
    sign = np.where((np.arange(HEAD_DIM) % 32) < 16, -1.0, 1.0).astype(np.float32)
    sin = sin * sign[None, :]
    cos = np.concatenate([cos, np.ones((extra_rows, HEAD_DIM), np.float32)], axis=0)
    sin = np.concatenate([sin, np.zeros((extra_rows, HEAD_DIM), np.float32)], axis=0)
    reps = LANES // HEAD_DIM
    return jnp.asarray(np.tile(cos, (1, reps))), jnp.asarray(np.tile(sin, (1, reps)))


def _forward(x, c, ctx, c_ctx, ada_w, ada_b, w_in, w_out, q_gain, k_gain, hy_conv_w, hy_conv_b,
             hy_w1, hy_b1, hy_freq, hy_w2, hy_b2, hy_w3, hy_b3, hy_d, na_rpb, ln1_g, ln1_b,
             ln2_g, ln2_b, router_w, router_b, exp_w1, exp_b1, exp_w2, exp_b2, *, grid_w):
    nb, n, d = x.shape
    m = ctx.shape[1]
    depth = ada_w.shape[0]
    n_experts = router_w.shape[-1]
    alpha = (2.0 * depth) ** 0.25
    tm = TOKEN_TILE
    rt = ROUTE_TILE
    assert n % tm == 0 and (nb * m) % tm == 0 and n % m == 0 and n_experts <= LANES
    assert n % (2 * rt) == 0 and (nb * m) % (2 * rt) == 0
    t_lat, t_all = nb * n, nb * (n + m)

    xf = jnp.concatenate([x.reshape(nb * n, d), ctx.reshape(nb * m, d)], axis=0)
    cvec = jnp.zeros((SUBLANES, d), F32).at[:nb].set(c).at[nb].set(c_ctx)
    mods = _adaln(cvec, ada_w, ada_b).reshape(depth, SUBLANES, 6, d)

    tiles = np.arange(t_all // tm)
    lat_tile = tiles < t_lat // tm
    tile_mod = jnp.asarray(np.where(lat_tile, tiles // (n // tm), nb), jnp.int32)
    tile_pos = jnp.asarray(np.where(lat_tile, tiles % (n // tm), n // tm), jnp.int32)
    rtiles = np.arange(t_all // rt)
    route_mod = jnp.asarray(np.where(rtiles < t_lat // rt, rtiles // (n // rt), nb), jnp.int32)
    cos_t, sin_t = _rope_tables(n, grid_w, tm)
    gmat = jnp.asarray(np.kron(np.eye(ATT_Q_HEADS), np.ones((HEAD_DIM, HEAD_DIM))), BF16)
    dft_lat = _dft_constants(n)
    dft_ctx = _dft_constants(m)

    for l in range(depth):
        need_ctx = l < depth - 1
        n_tok = t_all if need_ctx else t_lat
        q, k, v, hyp, nq, nk, nv = _proj(
            xf, mods[l], w_in[l].astype(BF16), jnp.tile(q_gain[l], ATT_Q_HEADS)[None, :],
            jnp.tile(k_gain[l], ATT_KV_HEADS)[None, :], gmat, cos_t, sin_t, tile_mod, tile_pos)
        y_att = _gqa(q, k, v, nb, n, m, need_ctx)
        filt = (hy_w1[l], hy_b1[l], hy_freq[l], hy_w2[l], hy_b2[l], hy_w3[l], hy_b3[l])
        gre, gim = _hyena_filter_spectrum(n, dft_lat, *filt)
        cb = hy_conv_b[l][None, :]
        dsk = hy_d[l][None, :]
        y_hy = _hyena(hyp, hy_conv_w[l], cb, dsk, gre, gim, dft_lat, nb, n, 0)
        if need_ctx:
            gre_c, gim_c = _hyena_filter_spectrum(m, dft_ctx, *filt)
            y_hy_c = _hyena(hyp, hy_conv_w[l], cb, dsk, gre_c, gim_c, dft_ctx, nb, m, t_lat // m)
            y_hy = jnp.concatenate([y_hy, y_hy_c], axis=0)
        bias_tab = _na_bias_table(na_rpb[l], grid_w, n // grid_w)
        y_na = _na(nq, nk, nv, bias_tab, nb, n, m, grid_w, need_ctx)

        router_wp = jnp.zeros((d, LANES), F32).at[:, :n_experts].set(router_w[l])
        router_bp = jnp.zeros((1, LANES), F32).at[0, :n_experts].set(router_b[l])
        x1, h2p, idx, wts = _mix_out(xf, y_att, y_hy, y_na, mods[l], w_out[l].astype(BF16), ln1_g[l][None, :],
                                     ln1_b[l][None, :], router_wp, router_bp, tile_mod, n_tok, alpha, n_experts)
        slots, tile_expert, n_used, n_slots = _routing_tables(idx[:, :TOP_K], n_experts, MOE_TILE)
        xs = _dispatch(h2p, slots, n_slots)
        ys = _experts(xs, tile_expert, n_used, exp_w1.reshape((-1,) + exp_w1.shape[2:]), exp_b1[l],
                      exp_w2.reshape((-1,) + exp_w2.shape[2:]), exp_b2[l], l)
        xf = _combine(x1, wts, slots, ys, mods[l], ln2_g[l][None, :], ln2_b[l][None, :], route_mod, alpha)
    return xf[:t_lat].reshape(nb, n, d)


def kernel(x, c, ctx, c_ctx, ada_w, ada_b, w_in, w_out, q_gain, k_gain, hy_conv_w, hy_conv_b, hy_w1, hy_b1,
           hy_freq, hy_w2, hy_b2, hy_w3, hy_b3, hy_d, na_rpb, ln1_g, ln1_b, ln2_g, ln2_b, router_w, router_b,
           exp_w1, exp_b1, exp_w2, exp_b2):
    return _forward(x, c, ctx, c_ctx, ada_w, ada_b, w_in, w_out, q_gain, k_gain, hy_conv_w, hy_conv_b, hy_w1,
                    hy_b1, hy_freq, hy_w2, hy_b2, hy_w3, hy_b3, hy_d, na_rpb, ln1_g, ln1_b, ln2_g, ln2_b,
                    router_w, router_b, exp_w1, exp_b1, exp_w2, exp_b2, grid_w=64)
```

```python
import cmath
import functools
import math

import jax
import jax.numpy as jnp
import numpy as np
from jax import lax
from jax.experimental import pallas as pl
from jax.experimental.pallas import tpu as pltpu

F32 = jnp.float32
BF16 = jnp.bfloat16
HIGHEST = lax.Precision.HIGHEST

HEAD_DIM = 64
ATT_Q_HEADS = 8
ATT_KV_HEADS = 2
ATT_GROUP = ATT_Q_HEADS // ATT_KV_HEADS
ATT_W = ATT_Q_HEADS * HEAD_DIM
KV_W = ATT_KV_HEADS * HEAD_DIM
HY_W = 256
HY_SHORT = 3
HY_EMB = 33
HY_BANDS = (HY_EMB - 1) // 2
HY_FO = 64
HY_MIN_DECAY = math.log(1e-2) / 1.5
HY_MAX_DECAY = math.log(1e-2) / 0.3
NA_HEADS = 4
NA_W = NA_HEADS * HEAD_DIM
NA_WIN_ROWS = 8
NA_WIN_COLS = 16
D_MIX = ATT_W + HY_W + NA_W
D_IN = ATT_W + 2 * KV_W + 3 * HY_W + 3 * NA_W
TOP_K = 4
SWIGLU_ALPHA = 1.702
SWIGLU_LIMIT = 7.0
ROPE_THETA = 10000.0
NORM_EPS = 1e-6
LOG2E = 1.0 / math.log(2.0)
Q_SCALE = HEAD_DIM ** -0.5 * LOG2E

LANES = 128
SUBLANES = 8
NEG_BIG = -1e30

TOKEN_TILE = 512
ROUTE_TILE = 256
ATT_Q_TILE = 256
NA_Q_ROWS = 8
NA_K_ROWS = 16
FFT_N1 = 8
HY_CB = 128
FREQ_CHUNK = 256
MOE_TILE = 512
VMEM_LIMIT = 56 * 1024 * 1024


def _cparams(sem, vmem=VMEM_LIMIT):
    return pltpu.CompilerParams(dimension_semantics=sem, vmem_limit_bytes=vmem)


def _layer_norm(x):
    mu = jnp.mean(x, axis=-1, keepdims=True)
    xc = x - mu
    var = jnp.mean(xc * xc, axis=-1, keepdims=True)
    return xc * lax.rsqrt(var + NORM_EPS)


def _dot(a, b):
    return jnp.dot(a, b, preferred_element_type=F32)


def _dot_nt(a, b):
    return lax.dot_general(a, b, (((1,), (1,)), ((), ())), preferred_element_type=F32)


def _split_bf16(x):
    hi = x.astype(BF16)
    lo = (x - hi.astype(F32)).astype(BF16)
    return hi, lo


def _dot3(a_hi, a_lo, b_hi, b_lo):
    return _dot(a_hi, b_hi) + _dot(a_hi, b_lo) + _dot(a_lo, b_hi)


def _adaln_kernel(c_ref, w_ref, b_ref, o_ref):
    c = c_ref[...]
    a = c * jax.nn.sigmoid(c)
    o_ref[0] = jnp.dot(a, w_ref[0], preferred_element_type=F32, precision=HIGHEST) + b_ref[0]


def _adaln(cvec, ada_w, ada_b):
    n_layers, d, d6 = ada_w.shape
    bn = d6 // 4 if (d6 // 4) % LANES == 0 else d6
    return pl.pallas_call(
        _adaln_kernel,
        grid=(n_layers, d6 // bn),
        in_specs=[
            pl.BlockSpec((SUBLANES, d), lambda l, j: (0, 0)),
            pl.BlockSpec((1, d, bn), lambda l, j: (l, 0, j)),
            pl.BlockSpec((1, 1, bn), lambda l, j: (l, 0, j)),
        ],
        out_specs=pl.BlockSpec((1, SUBLANES, bn), lambda l, j: (l, 0, j)),
        out_shape=jax.ShapeDtypeStruct((n_layers, SUBLANES, d6), F32),
        compiler_params=_cparams(("arbitrary", "arbitrary")),
        name="adaln",
    )(cvec, ada_w, ada_b.reshape(n_layers, 1, d6))


def _head_norm(a, g_ref, gain):
    w = a.shape[1]
    sq = a * a
    hi, lo = _split_bf16(sq)
    g = g_ref[0:w, 0:w]
    ss = _dot(hi, g) + _dot(lo, g)
    return a * lax.rsqrt(ss * (1.0 / HEAD_DIM) + NORM_EPS) * gain


def _rope(a, cos, sin_signed):
    w = a.shape[1]
    reps = w // LANES
    c = jnp.concatenate([cos] * reps, axis=1) if reps > 1 else cos
    s = jnp.concatenate([sin_signed] * reps, axis=1) if reps > 1 else sin_signed
    lane = lax.broadcasted_iota(jnp.int32, a.shape, 1)
    first = (lane & 31) < 16
    rot = jnp.where(first, pltpu.roll(a, w - 16, axis=1), pltpu.roll(a, 16, axis=1))
    return a * c + rot * s


def _proj_kernel(tmod_ref, tpos_ref, x_ref, mod_ref, w_ref, qg_ref, kg_ref, g_ref, cos_ref, sin_ref,
                 q_out, k_out, v_out, hy_out, nq_out, nk_out, nv_out):
    del tmod_ref, tpos_ref
    h = _layer_norm(x_ref[...]) * (1.0 + mod_ref[0, 1:2, :]) + mod_ref[0, 0:1, :]
    p = _dot(h.astype(BF16), w_ref[...])
    o = 0
    aq = p[:, o:o + ATT_W]; o += ATT_W
    ak = p[:, o:o + KV_W]; o += KV_W
    av = p[:, o:o + KV_W]; o += KV_W
    hy = p[:, o:o + 3 * HY_W]; o += 3 * HY_W
    nq = p[:, o:o + NA_W]; o += NA_W
    nk = p[:, o:o + NA_W]; o += NA_W
    nv = p[:, o:o + NA_W]
    cos = cos_ref[...]
    sin = sin_ref[...]
    q = _rope(_head_norm(aq, g_ref, qg_ref[...]), cos, sin) * Q_SCALE
    k = _rope(_head_norm(ak, g_ref, kg_ref[...]), cos, sin)
    q_out[...] = q.astype(BF16)
    k_out[...] = k.astype(BF16)
    v_out[...] = av.astype(BF16)
    hy_out[...] = hy
    nq_out[...] = (nq * Q_SCALE).astype(BF16)
    nk_out[...] = nk.astype(BF16)
    nv_out[...] = nv.astype(BF16)


def _proj(xf, mods_l, w_in_bf, q_gain, k_gain, gmat, cos_t, sin_t, tile_mod, tile_pos):
    t, d = xf.shape
    tm = TOKEN_TILE
    n_tiles = t // tm
    row = lambda i, tmod, tpos: (i, 0)
    const = lambda i, tmod, tpos: (0, 0)
    widths = (ATT_W, KV_W, KV_W, 3 * HY_W, NA_W, NA_W, NA_W)
    dtypes = (BF16, BF16, BF16, F32, BF16, BF16, BF16)
    grid_spec = pltpu.PrefetchScalarGridSpec(
        num_scalar_prefetch=2,
        grid=(n_tiles,),
        in_specs=[
            pl.BlockSpec((tm, d), row),
            pl.BlockSpec((1, 6, d), lambda i, tmod, tpos: (tmod[i], 0, 0)),
            pl.BlockSpec((d, D_IN), const),
            pl.BlockSpec((1, ATT_W), const),
            pl.BlockSpec((1, KV_W), const),
            pl.BlockSpec((ATT_W, ATT_W), const),
            pl.BlockSpec((tm, LANES), lambda i, tmod, tpos: (tpos[i], 0)),
            pl.BlockSpec((tm, LANES), lambda i, tmod, tpos: (tpos[i], 0)),
        ],
        out_specs=[pl.BlockSpec((tm, w), row) for w in widths],
    )
    return pl.pallas_call(
        _proj_kernel,
        grid_spec=grid_spec,
        out_shape=[jax.ShapeDtypeStruct((t, w), dt) for w, dt in zip(widths, dtypes)],
        compiler_params=_cparams(("arbitrary",)),
        name="ln_mod_proj",
    )(tile_mod, tile_pos, xf, mods_l, w_in_bf, q_gain, k_gain, gmat, cos_t, sin_t)


def _attend(q, segs, n_kv, grp, bias=None):
    tq = q.shape[0]
    hd = HEAD_DIM
    outs = []
    for j in range(n_kv):
        heads = [q[:, (j * grp + g) * hd:(j * grp + g + 1) * hd] for g in range(grp)]
        qs = jnp.concatenate(heads, axis=0) if grp > 1 else heads[0]
        scores = []
        for si, (k, _) in enumerate(segs):
            s = _dot_nt(qs, k[:, j * hd:(j + 1) * hd])
            if bias is not None and bias[j][si] is not None:
                s = s + bias[j][si]
            scores.append(s)
        m = scores[0].max(axis=-1, keepdims=True)
        for s in scores[1:]:
            m = jnp.maximum(m, s.max(axis=-1, keepdims=True))
        denom = None
        acc = None
        for s, (_, v) in zip(scores, segs):
            p = jnp.exp2(s - m)
            ps = p.sum(axis=-1, keepdims=True)
            pv = _dot(p.astype(BF16), v[:, j * hd:(j + 1) * hd])
            denom = ps if denom is None else denom + ps
            acc = pv if acc is None else acc + pv
        o = acc / denom
        for g in range(grp):
            outs.append(o[g * tq:(g + 1) * tq])
    return jnp.concatenate(outs, axis=1)


def _gqa_kernel(q_ref, k_ref, v_ref, o_ref):
    segs = [(k_ref[...], v_ref[...])]
    o_ref[...] = _attend(q_ref[...], segs, ATT_KV_HEADS, ATT_GROUP).astype(o_ref.dtype)


def _gqa(q, k, v, nb, n, m, with_ctx):
    tq = ATT_Q_TILE
    n_lat = n // tq
    qmap = lambda b, i: (b * n_lat + i, 0)
    seq = lambda b, i: (b, 0)

    def per_batch_keys(a):
        w = a.shape[1]
        lat, ctx = a[:nb * n].reshape(nb, n, w), a[nb * n:].reshape(nb, m, w)
        return jnp.concatenate([lat, ctx], axis=1).reshape(nb * (n + m), w)

    y = pl.pallas_call(
        _gqa_kernel,
        grid=(nb, n_lat),
        in_specs=[
            pl.BlockSpec((tq, ATT_W), qmap),
            pl.BlockSpec((n + m, KV_W), seq),
            pl.BlockSpec((n + m, KV_W), seq),
        ],
        out_specs=pl.BlockSpec((tq, ATT_W), qmap),
        out_shape=jax.ShapeDtypeStruct((nb * n, ATT_W), BF16),
        compiler_params=_cparams(("arbitrary", "arbitrary")),
        name="gqa_attention",
    )(q, per_batch_keys(k), per_batch_keys(v))
    if not with_ctx:
        return y
    yc = _ctx_attention(q, k, v, nb, n, m, ATT_KV_HEADS, ATT_GROUP, ATT_W, "gqa_ctx_attention")
    return jnp.concatenate([y, yc], axis=0)


def _na_kernel(q_ref, kl_ref, vl_ref, kc_ref, vc_ref, bias_ref, o_ref, *, grid_w, grid_rows):
    i = pl.program_id(1)
    krow0 = jnp.clip(i * NA_Q_ROWS - NA_WIN_ROWS // 2, 0, grid_rows - NA_K_ROWS)
    start = pl.multiple_of(krow0 * grid_w, 4 * grid_w)
    nk = NA_K_ROWS * grid_w
    segs = [(kl_ref[pl.ds(start, nk), :], vl_ref[pl.ds(start, nk), :]), (kc_ref[...], vc_ref[...])]
    bias = [[bias_ref[0, h], None] for h in range(NA_HEADS)]
    o_ref[...] = _attend(q_ref[...], segs, NA_HEADS, 1, bias).astype(o_ref.dtype)


def _na_bias_table(rpb, grid_w, grid_rows):
    qr, kr = NA_Q_ROWS, NA_K_ROWS
    n_rr, n_rc = 2 * NA_WIN_ROWS - 1, 2 * NA_WIN_COLS - 1
    c = np.arange(grid_w)[:, None]
    kcol = np.arange(grid_w)[None, :]
    cs = np.clip(c - NA_WIN_COLS // 2, 0, grid_w - NA_WIN_COLS)
    col_ok = (kcol >= cs) & (kcol < cs + NA_WIN_COLS)
    csel = ((kcol - c + NA_WIN_COLS - 1)[:, :, None] == np.arange(n_rc)) & col_ok[:, :, None]
    rsel, mask = [], []
    for r0 in (0, qr, grid_rows - qr):
        k0 = int(np.clip(r0 - NA_WIN_ROWS // 2, 0, grid_rows - kr))
        r = r0 + np.arange(qr)[:, None]
        krow = k0 + np.arange(kr)[None, :]
        rs = np.clip(r - NA_WIN_ROWS // 2, 0, grid_rows - NA_WIN_ROWS)
        row_ok = (krow >= rs) & (krow < rs + NA_WIN_ROWS)
        rsel.append(((krow - r + NA_WIN_ROWS - 1)[:, :, None] == np.arange(n_rr)) & row_ok[:, :, None])
        ok = row_ok[:, None, :, None] & col_ok[None, :, None, :]
        mask.append(np.where(ok, 0.0, NEG_BIG))
    rsel = jnp.asarray(np.stack(rsel), F32)
    mask = jnp.asarray(np.stack(mask), F32)
    vals = jnp.einsum("tjia,hab,ckb->thjcik", rsel, rpb, jnp.asarray(csel, F32), precision=HIGHEST)
    return (vals * LOG2E + mask[:, None]).reshape(3, NA_HEADS, qr * grid_w, kr * grid_w)


def _na(nq, nk, nv, bias_tab, nb, n, m, grid_w, with_ctx):
    t = nq.shape[0]
    grid_rows = n // grid_w
    tq = NA_Q_ROWS * grid_w
    n_lat = n // tq
    ctx_blk0 = nb * n // m
    lat = lambda b, i: (b, 0)
    ctx = lambda b, i: (ctx_blk0 + b, 0)
    qmap = lambda b, i: (b * n_lat + i, 0)

    def bmap(b, i):
        return (jnp.where(i == 0, 0, jnp.where(i >= n_lat - 1, 2, 1)), 0, 0, 0)

    y = pl.pallas_call(
        functools.partial(_na_kernel, grid_w=grid_w, grid_rows=grid_rows),
        grid=(nb, n_lat),
        in_specs=[
            pl.BlockSpec((tq, NA_W), qmap),
            pl.BlockSpec((n, NA_W), lat),
            pl.BlockSpec((n, NA_W), lat),
            pl.BlockSpec((m, NA_W), ctx),
            pl.BlockSpec((m, NA_W), ctx),
            pl.BlockSpec((1, NA_HEADS, tq, NA_K_ROWS * grid_w), bmap),
        ],
        out_specs=pl.BlockSpec((tq, NA_W), qmap),
        out_shape=jax.ShapeDtypeStruct((nb * n, NA_W), BF16),
        compiler_params=_cparams(("arbitrary", "arbitrary")),
        name="neighbourhood_attention",
    )(nq, nk, nv, nk, nv, bias_tab)
    if not with_ctx:
        return y
    yc = _ctx_attention(nq, nk, nv, nb, n, m, NA_HEADS, 1, NA_W, "na_ctx_attention")
    return jnp.concatenate([y, yc], axis=0)


def _ctx_attn_kernel(q_ref, k_ref, v_ref, o_ref, *, n_kv, grp):
    segs = [(k_ref[...], v_ref[...])]
    o_ref[...] = _attend(q_ref[...], segs, n_kv, grp).astype(o_ref.dtype)


def _ctx_attention(q, k, v, nb, n, m, n_kv, grp, width, name):
    ctx_blk0 = nb * n // m
    ctx = lambda b: (ctx_blk0 + b, 0)
    kw = n_kv * HEAD_DIM
    return pl.pallas_call(
        functools.partial(_ctx_attn_kernel, n_kv=n_kv, grp=grp),
        grid=(nb,),
        in_specs=[
            pl.BlockSpec((m, width), ctx),
            pl.BlockSpec((m, kw), ctx),
            pl.BlockSpec((m, kw), ctx),
        ],
        out_specs=pl.BlockSpec((m, width), lambda b: (b, 0)),
        out_shape=jax.ShapeDtypeStruct((nb * m, width), BF16),
        compiler_params=_cparams(("arbitrary",)),
        name=name,
    )(q, k, v)


def _dft_constants(seq_len):
    n = 2 * seq_len
    n2 = n // FFT_N1
    idx = np.arange(n2, dtype=np.int64)
    ang = 2.0 * np.pi * ((idx[:, None] * idx[None, :]) % n2).astype(np.float64) / n2
    cmat, smat = np.cos(ang), np.sin(ang)

    def split(a):
        hi = jnp.asarray(a, F32).astype(BF16)
        lo = (jnp.asarray(a, F32) - hi.astype(F32)).astype(BF16)
        return hi, lo

    def col_chunks(a):
        return np.transpose(a.reshape(a.shape[0], n2 // chunk, chunk), (1, 0, 2))

    chunk = min(FREQ_CHUNK, n2)
    tw_ang = 2.0 * np.pi * (idx[:, None] * np.arange(FFT_N1)[None, :]).astype(np.float64) / n
    return dict(
        n2=n2, chunk=chunk,
        fwd_c=split(cmat), fwd_s=split(smat),
        inv_c=split(col_chunks(cmat[: n2 // 2] / n)),
        inv_s=split(col_chunks(smat[: n2 // 2] / n)),
        tw_re=jnp.asarray(np.cos(tw_ang), F32), tw_im=jnp.asarray(-np.sin(tw_ang), F32),
    )


def _cmul_const(a, w):
    re, im = a
    if abs(w - 1) < 1e-12:
        return a
    if abs(w + 1) < 1e-12:
        return (-re, -im)
    if abs(w + 1j) < 1e-12:
        return (im, -re)
    if abs(w - 1j) < 1e-12:
        return (-im, re)
    return (re * w.real - im * w.imag, re * w.imag + im * w.real)


def _fft_blocks(xs, sign):
    n = len(xs)
    if n == 1:
        return xs
    even = _fft_blocks(xs[0::2], sign)
    odd = _fft_blocks(xs[1::2], sign)
    out = [None] * n
    for k in range(n // 2):
        t = _cmul_const(odd[k], cmath.exp(sign * 2j * cmath.pi * k / n))
        out[k] = (even[k][0] + t[0], even[k][1] + t[1])
        out[k + n // 2] = (even[k][0] - t[0], even[k][1] - t[1])
    return out


def _lane_blocks(a):
    return [a[:, j * LANES:(j + 1) * LANES] for j in range(a.shape[1] // LANES)]


def _forward_spectrum(xh, xl, rows, ch_ref, cl_ref, sh_ref, sl_ref, twr_ref, twi_ref):
    a_re = _dot3(ch_ref[rows, :], cl_ref[rows, :], xh, xl)
    a_im = -_dot3(sh_ref[rows, :], sl_ref[rows, :], xh, xl)
    blocks = []
    for s1, (re, im) in enumerate(zip(_lane_blocks(a_re), _lane_blocks(a_im))):
        tr = twr_ref[rows, s1:s1 + 1]
        ti = twi_ref[rows, s1:s1 + 1]
        blocks.append((re * tr - im * ti, re * ti + im * tr))
    return _fft_blocks(blocks, -1)


def _filter_kernel(emb_ref, w1_ref, b1_ref, fr_ref, w2_ref, b2_ref, w3f_ref, w3b_ref, b3f_ref, b3b_ref,
                   dec_ref, ch_ref, cl_ref, sh_ref, sl_ref, twr_ref, twi_ref, gre_ref, gim_ref, g_scr, hid_scr, *,
                   seq_len, chunk):
    n = 2 * seq_len
    n2 = n // FFT_N1
    def mm(a, w_ref):
        return _dot3(*_split_bf16(a), *_split_bf16(w_ref[...]))

    @pl.when(pl.program_id(0) == 0)
    def _():
        hid1 = jnp.sin(fr_ref[0:1, :] * (mm(emb_ref[...], w1_ref) + b1_ref[...]))
        hid_scr[...] = jnp.sin(fr_ref[1:2, :] * (mm(hid1, w2_ref) + b2_ref[...]))

    hid = hid_scr[...]
    f_fwd = mm(hid, w3f_ref) + b3f_ref[...]
    f_bwd = mm(hid, w3b_ref) + b3b_ref[...]
    row = lax.broadcasted_iota(jnp.int32, f_fwd.shape, 0)
    t_pos = emb_ref[:, 0:1]
    g = jnp.where(row < seq_len, f_fwd, jnp.where(row > seq_len, f_bwd, 0.0))
    g = g * jnp.exp(-t_pos * jnp.abs(dec_ref[...]))
    g = g * lax.rsqrt(jnp.sum(g * g, axis=0, keepdims=True) + NORM_EPS)
    g_scr[...] = g
    x = jnp.concatenate([g_scr[pl.ds(s1, n2, stride=FFT_N1), :] for s1 in range(FFT_N1)], axis=1)
    xh, xl = _split_bf16(x)

    def freq_chunk(ci, carry):
        rows = pl.ds(pl.multiple_of(ci * chunk, chunk), chunk)
        spec = _forward_spectrum(xh, xl, rows, ch_ref, cl_ref, sh_ref, sl_ref, twr_ref, twi_ref)
        gre_ref[0, rows, :] = jnp.concatenate([b[0] for b in spec], axis=1)
        gim_ref[0, rows, :] = jnp.concatenate([b[1] for b in spec], axis=1)
        return carry

    lax.fori_loop(0, n2 // chunk, freq_chunk, 0)


def _filter_embedding(seq_len):
    n = 2 * seq_len
    pos = np.concatenate([np.arange(seq_len), [0], np.arange(seq_len - 1, 0, -1)])
    t = np.linspace(0.0, 1.0, seq_len, dtype=np.float32)[pos][:, None]
    w = ((2.0 * math.pi / seq_len) * np.arange(seq_len, dtype=np.float32))[pos][:, None]
    bands = np.linspace(1e-4, HY_BANDS - 1, HY_BANDS, dtype=np.float32)[None, :]
    z = np.concatenate([t, np.cos(bands * w), -np.sin(bands * w)], axis=-1).astype(np.float32)
    emb = np.zeros((n, LANES), np.float32)
    emb[:, :HY_EMB] = z
    return jnp.asarray(emb)


def _hyena_filter_spectrum(seq_len, consts, f_w1, f_b1, f_freq, f_w2, f_b2, f_w3, f_b3):
    n = 2 * seq_len
    n2 = consts["n2"]
    ncb = HY_W // HY_CB
    emb = _filter_embedding(seq_len)
    w1p = jnp.zeros((LANES, HY_FO), F32).at[:HY_EMB].set(f_w1)
    deltas = jnp.asarray(np.linspace(HY_MIN_DECAY, HY_MAX_DECAY, HY_W, dtype=np.float32))[None, :]
    const = lambda j: (0, 0)
    out_sds = jax.ShapeDtypeStruct((ncb, n2, FFT_N1 * HY_CB), F32)
    return pl.pallas_call(
        functools.partial(_filter_kernel, seq_len=seq_len, chunk=consts["chunk"]),
        grid=(ncb,),
        in_specs=[
            pl.BlockSpec((n, LANES), const),
            pl.BlockSpec((LANES, HY_FO), const),
            pl.BlockSpec((1, HY_FO), const),
            pl.BlockSpec((2, HY_FO), const),
            pl.BlockSpec((HY_FO, HY_FO), const),
            pl.BlockSpec((1, HY_FO), const),
            pl.BlockSpec((HY_FO, HY_CB), lambda j: (0, j)),
            pl.BlockSpec((HY_FO, HY_CB), lambda j: (0, ncb + j)),
            pl.BlockSpec((1, HY_CB), lambda j: (0, j)),
            pl.BlockSpec((1, HY_CB), lambda j: (0, ncb + j)),
            pl.BlockSpec((1, HY_CB), lambda j: (0, j)),
        ] + [pl.BlockSpec((n2, n2), const, pipeline_mode=pl.Buffered(1))] * 4
          + [pl.BlockSpec((n2, FFT_N1), const, pipeline_mode=pl.Buffered(1))] * 2,
        out_specs=[pl.BlockSpec((1, n2, FFT_N1 * HY_CB), lambda j: (j, 0, 0))] * 2,
        out_shape=[out_sds, out_sds],
        scratch_shapes=[pltpu.VMEM((n, HY_CB), F32), pltpu.VMEM((n, HY_FO), F32)],
        compiler_params=_cparams(("arbitrary",)),
        name=f"hyena_filter_{seq_len}",
    )(emb, w1p, f_b1[None, :], f_freq, f_w2, f_b2[None, :], f_w3, f_w3, f_b3[None, :], f_b3[None, :], deltas,
      *consts["fwd_c"], *consts["fwd_s"], consts["tw_re"], consts["tw_im"])


def _short_conv(p, w_ref, b_ref):
    n = p.shape[0]
    row = lax.broadcasted_iota(jnp.int32, p.shape, 0)
    prev = jnp.where(row == 0, 0.0, pltpu.roll(p, 1, axis=0))
    nxt = jnp.where(row == n - 1, 0.0, pltpu.roll(p, n - 1, axis=0))
    return prev * w_ref[0:1, :] + p * w_ref[1:2, :] + nxt * w_ref[2:3, :] + b_ref[...]


def _hyena_kernel(p0_ref, p1_ref, p2_ref, w0_ref, w1_ref, w2_ref, b0_ref, b1_ref, b2_ref, d_ref,
                  gre_ref, gim_ref, fch_ref, fcl_ref, fsh_ref, fsl_ref, ich_ref, icl_ref, ish_ref, isl_ref,
                  twr_ref, twi_ref, o_ref, z_scr, y_scr, acc_scr, *, seq_len, chunk):
    n2 = 2 * seq_len // FFT_N1
    half = n2 // 2
    x1 = _short_conv(p1_ref[...], w1_ref, b1_ref)
    v = _short_conv(p2_ref[...], w2_ref, b2_ref)
    z_scr[...] = x1 * v
    x = jnp.concatenate([z_scr[pl.ds(s1, half, stride=FFT_N1), :] for s1 in range(FFT_N1)], axis=1)
    xh, xl = _split_bf16(x)
    acc_scr[...] = jnp.zeros(acc_scr.shape, F32)

    def freq_chunk(ci, carry):
        rows = pl.ds(pl.multiple_of(ci * chunk, chunk), chunk)
        zf = _forward_spectrum(xh, xl, rows, fch_ref, fcl_ref, fsh_ref, fsl_ref, twr_ref, twi_ref)
        gre = _lane_blocks(gre_ref[0, rows, :])
        gim = _lane_blocks(gim_ref[0, rows, :])
        prod = [(zr * gr - zi * gi, zr * gi + zi * gr) for (zr, zi), gr, gi in zip(zf, gre, gim)]
        back = _fft_blocks(prod, +1)
        c_re, c_im = [], []
        for s1, (re, im) in enumerate(back):
            tr = twr_ref[rows, s1:s1 + 1]
            ti = -twi_ref[rows, s1:s1 + 1]
            c_re.append(re * tr - im * ti)
            c_im.append(re * ti + im * tr)
        rh, rl = _split_bf16(jnp.concatenate(c_re, axis=1))
        ih, il = _split_bf16(jnp.concatenate(c_im, axis=1))
        acc_scr[...] += _dot3(ich_ref[ci], icl_ref[ci], rh, rl) - _dot3(ish_ref[ci], isl_ref[ci], ih, il)
        return carry

    lax.fori_loop(0, n2 // chunk, freq_chunk, 0)
    for s1 in range(FFT_N1):
        y_scr[pl.ds(s1, half, stride=FFT_N1), :] = acc_scr[:, s1 * LANES:(s1 + 1) * LANES]
    x0 = _short_conv(p0_ref[...], w0_ref, b0_ref)
    o_ref[...] = (x0 * (y_scr[...] + z_scr[...] * d_ref[...])).astype(o_ref.dtype)


def _hyena(hyp, conv_w, conv_b, d_skip, gre, gim, consts, nb, seq_len, row_blk0):
    n2 = consts["n2"]
    chunk = consts["chunk"]
    ncb = HY_W // HY_CB
    half = n2 // 2
    const = lambda j, b: (0, 0)
    once = pl.Buffered(1)
    in_specs = (
        [pl.BlockSpec((seq_len, HY_CB), (lambda j, b, g=g: (row_blk0 + b, g * ncb + j))) for g in range(3)]
        + [pl.BlockSpec((HY_SHORT, HY_CB), (lambda j, b, g=g: (0, g * ncb + j))) for g in range(3)]
        + [pl.BlockSpec((1, HY_CB), (lambda j, b, g=g: (0, g * ncb + j))) for g in range(3)]
        + [pl.BlockSpec((1, HY_CB), lambda j, b: (0, j))]
        + [pl.BlockSpec((1, n2, FFT_N1 * HY_CB), lambda j, b: (j, 0, 0), pipeline_mode=once)] * 2
        + [pl.BlockSpec((n2, half), const, pipeline_mode=once)] * 4
        + [pl.BlockSpec((n2 // chunk, half, chunk), lambda j, b: (0, 0, 0), pipeline_mode=once)] * 4
        + [pl.BlockSpec((n2, FFT_N1), const, pipeline_mode=once)] * 2
    )
    fwd = [a[:, :half] for a in (*consts["fwd_c"], *consts["fwd_s"])]
    args = [hyp, hyp, hyp, conv_w, conv_w, conv_w, conv_b, conv_b, conv_b, d_skip, gre, gim,
            *fwd, *consts["inv_c"], *consts["inv_s"], consts["tw_re"], consts["tw_im"]]
    return pl.pallas_call(
        functools.partial(_hyena_kernel, seq_len=seq_len, chunk=chunk),
        grid=(ncb, nb),
        in_specs=in_specs,
        out_specs=pl.BlockSpec((seq_len, HY_CB), lambda j, b: (b, j)),
        out_shape=jax.ShapeDtypeStruct((nb * seq_len, HY_W), BF16),
        scratch_shapes=[pltpu.VMEM((seq_len, HY_CB), F32), pltpu.VMEM((seq_len, HY_CB), F32),
                        pltpu.VMEM((half, FFT_N1 * HY_CB), F32)],
        compiler_params=_cparams(("arbitrary", "arbitrary")),
        name=f"hyena_mixer_{seq_len}",
    )(*args)


def _pack_pairs(h):
    half = h.shape[1] // 2
    lo = pltpu.bitcast(h[:, :half].astype(BF16).astype(F32), jnp.uint32) >> 16
    hi = pltpu.bitcast(h[:, half:].astype(BF16).astype(F32), jnp.uint32) & jnp.uint32(0xFFFF0000)
    return lo | hi


def _unpack_pairs(u):
    lo = pltpu.bitcast(u << 16, F32)
    hi = pltpu.bitcast(u & jnp.uint32(0xFFFF0000), F32)
    return jnp.concatenate([lo, hi], axis=1).astype(BF16)


def _mix_out_kernel(tmod_ref, x_ref, ya_ref, yh_ref, yn_ref, mod_ref, w_ref, g_ref, b_ref, rwh_ref, rwl_ref, rb_ref,
                    x1_out, h2_out, idx_out, wt_out, *, alpha, n_experts):
    del tmod_ref
    y = _dot(jnp.concatenate([ya_ref[...], yh_ref[...], yn_ref[...]], axis=1), w_ref[...])
    x1 = _layer_norm(alpha * x_ref[...] + mod_ref[0, 2:3, :] * y) * g_ref[...] + b_ref[...]
    x1_out[...] = x1
    h2 = _layer_norm(x1) * (1.0 + mod_ref[0, 4:5, :]) + mod_ref[0, 3:4, :]
    h2_out[...] = _pack_pairs(h2)
    h2_hi, h2_lo = _split_bf16(h2)
    logits = _dot3(h2_hi, h2_lo, rwh_ref[...], rwl_ref[...]) + rb_ref[...]
    lane = lax.broadcasted_iota(jnp.int32, logits.shape, 1)
    lane_f = lane.astype(F32)
    cur = jnp.where(lane < n_experts, logits, -jnp.inf)
    vals, idxs = [], []
    for _ in range(TOP_K):
        m = cur.max(axis=-1, keepdims=True)
        ix = jnp.where(cur == m, lane_f, float(LANES)).min(axis=-1, keepdims=True)
        vals.append(m)
        idxs.append(ix)
        cur = jnp.where(lane_f == ix, -jnp.inf, cur)
    es = [jnp.exp(v - vals[0]) for v in vals]
    denom = es[0]
    for e in es[1:]:
        denom = denom + e
    idx_v = jnp.zeros(logits.shape, F32)
    wt_v = jnp.zeros(logits.shape, F32)
    for k in range(TOP_K):
        idx_v = jnp.where(lane == k, idxs[k], idx_v)
        wt_v = jnp.where(lane == k, es[k] / denom, wt_v)
    idx_out[...] = idx_v.astype(jnp.int32)
    wt_out[...] = wt_v


def _mix_out(xf, y_att, y_hy, y_na, mods_l, w_out_bf, ln_g, ln_b, router_wp, router_bp, tile_mod, n_tok,
             alpha, n_experts):
    d = xf.shape[1]
    tm = TOKEN_TILE
    row = lambda i, tmod: (i, 0)
    const = lambda i, tmod: (0, 0)
    grid_spec = pltpu.PrefetchScalarGridSpec(
        num_scalar_prefetch=1,
        grid=(n_tok // tm,),
        in_specs=[
            pl.BlockSpec((tm, d), row),
            pl.BlockSpec((tm, ATT_W), row),
            pl.BlockSpec((tm, HY_W), row),
            pl.BlockSpec((tm, NA_W), row),
            pl.BlockSpec((1, 6, d), lambda i, tmod: (tmod[i], 0, 0)),
            pl.BlockSpec((D_MIX, d), const),
            pl.BlockSpec((1, d), const),
            pl.BlockSpec((1, d), const),
            pl.BlockSpec((d, LANES), const),
            pl.BlockSpec((d, LANES), const),
            pl.BlockSpec((1, LANES), const),
        ],
        out_specs=[
            pl.BlockSpec((tm, d), row),
            pl.BlockSpec((tm, d // 2), row),
            pl.BlockSpec((tm, LANES), row),
            pl.BlockSpec((tm, LANES), row),
        ],
    )
    return pl.pallas_call(
        functools.partial(_mix_out_kernel, alpha=alpha, n_experts=n_experts),
        grid_spec=grid_spec,
        out_shape=[
            jax.ShapeDtypeStruct((n_tok, d), F32),
            jax.ShapeDtypeStruct((n_tok, d // 2), jnp.uint32),
            jax.ShapeDtypeStruct((n_tok, LANES), jnp.int32),
            jax.ShapeDtypeStruct((n_tok, LANES), F32),
        ],
        compiler_params=_cparams(("arbitrary",)),
        name="mix_out_norm_route",
    )(tile_mod, xf, y_att, y_hy, y_na, mods_l, w_out_bf, ln_g, ln_b, *_split_bf16(router_wp), router_bp)


def _dispatch_kernel(slot_ref, h_ref, xs_in, xs_hbm, sem, *, tm):
    del xs_in

    def row_copy(src_row, dst_row, b):
        return pltpu.make_async_copy(h_ref.at[pl.ds(src_row, 1)], xs_hbm.at[pl.ds(dst_row, 1)], sem.at[b])

    def issue(b):
        def body(t, carry):
            for k in range(TOP_K):
                row_copy(b * tm + t, slot_ref[(b * tm + t) * TOP_K + k], b).start(priority=k % 2)
            return carry

        lax.fori_loop(0, tm, body, 0, unroll=8)

    def drain(b):
        def body(t, carry):
            for k in range(TOP_K):
                row_copy(0, 0, b).wait()
            return carry

        lax.fori_loop(0, tm, body, 0, unroll=8)

    issue(0)
    issue(1)
    drain(0)
    drain(1)


def _dispatch(h2p, slots_flat, n_slots):
    n_tok, hw = h2p.shape
    tm = ROUTE_TILE
    assert n_tok % (2 * tm) == 0
    xs0 = jnp.zeros((n_slots, hw), h2p.dtype)
    return pl.pallas_call(
        functools.partial(_dispatch_kernel, tm=tm),
        grid=(n_tok // (2 * tm),),
        in_specs=[
            pl.BlockSpec((2 * tm * TOP_K,), lambda i: (i,), memory_space=pltpu.SMEM),
            pl.BlockSpec((2 * tm, hw), lambda i: (i, 0)),
            pl.BlockSpec(memory_space=pl.ANY),
        ],
        out_specs=pl.BlockSpec(memory_space=pl.ANY),
        out_shape=jax.ShapeDtypeStruct((n_slots, hw), h2p.dtype),
        scratch_shapes=[pltpu.SemaphoreType.DMA((2,))],
        input_output_aliases={2: 0},
        compiler_params=_cparams(("arbitrary",)),
        name="moe_dispatch",
    )(slots_flat, h2p, xs0)


def _expert_kernel(te_ref, nu_ref, x_ref, w1_ref, b1_ref, w2_ref, b2_ref, o_ref, w1_bf, w2_bf, *, d_expert):
    i = pl.program_id(0)

    @pl.when((i == 0) | (te_ref[i] != te_ref[jnp.maximum(i - 1, 0)]))
    def _():
        w1_bf[...] = w1_ref[0].astype(BF16)
        w2_bf[...] = w2_ref[0].astype(BF16)

    @pl.when(i < nu_ref[0])
    def _():
        x = _unpack_pairs(x_ref[...])
        hid = _dot(x, w1_bf[...]) + b1_ref[0]
        glu = jnp.minimum(hid[:, :d_expert], SWIGLU_LIMIT)
        lin = jnp.clip(hid[:, d_expert:], -SWIGLU_LIMIT, SWIGLU_LIMIT)
        act = glu * jax.nn.sigmoid(SWIGLU_ALPHA * glu) * (lin + 1.0)
        o_ref[...] = _dot(act.astype(BF16), w2_bf[...]) + b2_ref[0]

    @pl.when(i >= nu_ref[0])
    def _():
        o_ref[...] = jnp.zeros(o_ref.shape, o_ref.dtype)


def _experts(xs, tile_expert, n_used, w1, b1, w2, b2, layer):
    n_slots, hw = xs.shape
    n_exp = b1.shape[0]
    _, d, d2 = w1.shape
    d_expert = d2 // 2
    tm = MOE_TILE
    n_tiles = n_slots // tm
    e0 = layer * n_exp
    grid_spec = pltpu.PrefetchScalarGridSpec(
        num_scalar_prefetch=2,
        grid=(n_tiles,),
        in_specs=[
            pl.BlockSpec((tm, hw), lambda i, te, nu: (i, 0)),
            pl.BlockSpec((1, d, d2), lambda i, te, nu: (e0 + te[i], 0, 0)),
            pl.BlockSpec((1, 1, d2), lambda i, te, nu: (te[i], 0, 0)),
            pl.BlockSpec((1, d_expert, d), lambda i, te, nu: (e0 + te[i], 0, 0)),
            pl.BlockSpec((1, 1, d), lambda i, te, nu: (te[i], 0, 0)),
        ],
        out_specs=pl.BlockSpec((tm, d), lambda i, te, nu: (i, 0)),
        scratch_shapes=[pltpu.VMEM((d, d2), BF16), pltpu.VMEM((d_expert, d), BF16)],
    )
    return pl.pallas_call(
        functools.partial(_expert_kernel, d_expert=d_expert),
        grid_spec=grid_spec,
        out_shape=jax.ShapeDtypeStruct((n_slots, d), F32),
        compiler_params=_cparams(("arbitrary",)),
        name="moe_experts",
    )(tile_expert, n_used, xs, w1, b1.reshape(n_exp, 1, d2), w2, b2.reshape(n_exp, 1, d))


def _combine_kernel(tmod_ref, slot_ref, next_ref, x1_ref, wt_ref, mod_ref, g_ref, b_ref, ys_hbm, o_ref, buf, sem,
                    *, tm, alpha):
    del tmod_ref
    i = pl.program_id(0)

    def row_copy(src_row, b, k, t):
        return pltpu.make_async_copy(ys_hbm.at[pl.ds(src_row, 1)], buf.at[b, k, pl.ds(t, 1)], sem.at[b])

    def issue(ref, off, b):
        def body(t, carry):
            for k in range(TOP_K):
                row_copy(ref[off + t * TOP_K + k], b, k, t).start(priority=k % 2)
            return carry

        lax.fori_loop(0, tm, body, 0, unroll=8)

    def drain(b):
        def body(t, carry):
            for k in range(TOP_K):
                row_copy(0, b, k, t).wait()
            return carry

        lax.fori_loop(0, tm, body, 0, unroll=8)

    def finish(b):
        rows = pl.ds(b * tm, tm)
        wt = wt_ref[rows, :]
        moe = buf[b, 0] * wt[:, 0:1]
        for k in range(1, TOP_K):
            moe = moe + buf[b, k] * wt[:, k:k + 1]
        o_ref[rows, :] = _layer_norm(alpha * x1_ref[rows, :] + mod_ref[0, 5:6, :] * moe) * g_ref[...] + b_ref[...]

    @pl.when(i == 0)
    def _():
        issue(slot_ref, 0, 0)

    issue(slot_ref, tm * TOP_K, 1)
    drain(0)
    finish(0)

    @pl.when(i + 1 < pl.num_programs(0))
    def _():
        issue(next_ref, 0, 0)

    drain(1)
    finish(1)


def _combine(x1, wts, slots_flat, ys, mods_l, ln_g, ln_b, tile_mod, alpha):
    n_tok, d = x1.shape
    tm = ROUTE_TILE
    n_steps = n_tok // (2 * tm)
    assert n_tok % (2 * tm) == 0
    row = lambda i, tmod: (i, 0)
    const = lambda i, tmod: (0, 0)
    grid_spec = pltpu.PrefetchScalarGridSpec(
        num_scalar_prefetch=1,
        grid=(n_steps,),
        in_specs=[
            pl.BlockSpec((2 * tm * TOP_K,), lambda i, tmod: (i,), memory_space=pltpu.SMEM),
            pl.BlockSpec((tm * TOP_K,), lambda i, tmod: (jnp.minimum(2 * i + 2, 2 * n_steps - 1),),
                         memory_space=pltpu.SMEM),
            pl.BlockSpec((2 * tm, d), row),
            pl.BlockSpec((2 * tm, LANES), row),
            pl.BlockSpec((1, 6, d), lambda i, tmod: (tmod[2 * i], 0, 0)),
            pl.BlockSpec((1, d), const),
            pl.BlockSpec((1, d), const),
            pl.BlockSpec(memory_space=pl.ANY),
        ],
        out_specs=pl.BlockSpec((2 * tm, d), row),
        scratch_shapes=[pltpu.VMEM((2, TOP_K, tm, d), F32), pltpu.SemaphoreType.DMA((2,))],
    )
    return pl.pallas_call(
        functools.partial(_combine_kernel, tm=tm, alpha=alpha),
        grid_spec=grid_spec,
        out_shape=jax.ShapeDtypeStruct((n_tok, d), F32),
        compiler_params=_cparams(("arbitrary",)),
        name="moe_combine_norm",
    )(tile_mod, slots_flat, slots_flat, x1, wts, mods_l, ln_g, ln_b, ys)


def _routing_tables(idx4, n_experts, tile):
    n_tok = idx4.shape[0]
    blk = ROUTE_TILE
    hit = idx4[:, :, None] == jnp.arange(n_experts, dtype=jnp.int32)[None, None, :]
    onehot = hit.any(axis=1)
    blocks = onehot.reshape(n_tok // blk, blk, n_experts).astype(BF16)
    tri = jnp.asarray(np.tril(np.ones((blk, blk), np.float32), -1), BF16)
    within = jnp.einsum("ij,bjk->bik", tri, blocks, preferred_element_type=F32).astype(jnp.int32)
    block_tot = onehot.reshape(n_tok // blk, blk, n_experts).sum(axis=1, dtype=jnp.int32)
    block_pre = jnp.cumsum(block_tot, axis=0) - block_tot
    pos = (within + block_pre[:, None, :]).reshape(n_tok, n_experts)
    counts = block_tot.sum(axis=0)
    padded = ((counts + tile - 1) // tile) * tile
    ends = jnp.cumsum(padded)
    offs = ends - padded
    slots = jnp.sum(jnp.where(hit, (offs[None, :] + pos)[:, None, :], 0), axis=2)
    n_tiles = (n_tok * TOP_K) // tile + n_experts
    tile_start = jnp.arange(n_tiles, dtype=jnp.int32) * tile
    tile_expert = jnp.minimum(jnp.sum(tile_start[:, None] >= ends[None, :], axis=1), n_experts - 1).astype(jnp.int32)
    n_used = (ends[-1] // tile).astype(jnp.int32)
    last = tile_expert[jnp.maximum(n_used - 1, 0)]
    tile_expert = jnp.where(jnp.arange(n_tiles) < n_used, tile_expert, last)
    return slots.astype(jnp.int32).reshape(-1), tile_expert, n_used.reshape(1), n_tiles * tile


def _rope_tables(n, grid_w, extra_rows):
    t = np.arange(n)
    row = (t // grid_w).astype(np.float32)[:, None]
    col = (t % grid_w).astype(np.float32)[:, None]
    axis_dim = HEAD_DIM // 2
    inv_freq = (ROPE_THETA ** (-np.arange(0, axis_dim, 2, dtype=np.float32) / axis_dim)).astype(np.float32)
    ang_r = row * inv_freq
    ang_c = col * inv_freq
    ang = np.concatenate([ang_r, ang_r, ang_c, ang_c], axis=-1)
    cos = np.cos(ang).astype(np.float32)
    sin = np.sin(ang).astype(np.float32)
    sign = np.where((np.arange(HEAD_DIM) % 32) < 16, -1.0, 1.0).astype(np.float32)
    sin = sin * sign[None, :]
    cos = np.concatenate([cos, np.ones((extra_rows, HEAD_DIM), np.float32)], axis=0)
    sin = np.concatenate([sin, np.zeros((extra_rows, HEAD_DIM), np.float32)], axis=0)
    reps = LANES // HEAD_DIM
    return jnp.asarray(np.tile(cos, (1, reps))), jnp.asarray(np.tile(sin, (1, reps)))


def _forward(x, c, ctx, c_ctx, ada_w, ada_b, w_in, w_out, q_gain, k_gain, hy_conv_w, hy_conv_b,
             hy_w1, hy_b1, hy_freq, hy_w2, hy_b2, hy_w3, hy_b3, hy_d, na_rpb, ln1_g, ln1_b,
             ln2_g, ln2_b, router_w, router_b, exp_w1, exp_b1, exp_w2, exp_b2, *, grid_w):
    nb, n, d = x.shape
    m = ctx.shape[1]
    depth = ada_w.shape[0]
    n_experts = router_w.shape[-1]
    alpha = (2.0 * depth) ** 0.25
    tm = TOKEN_TILE
    rt = ROUTE_TILE
    assert n % tm == 0 and (nb * m) % tm == 0 and n % m == 0 and n_experts <= LANES
    assert n % (2 * rt) == 0 and (nb * m) % (2 * rt) == 0
    t_lat, t_all = nb * n, nb * (n + m)

    xf = jnp.concatenate([x.reshape(nb * n, d), ctx.reshape(nb * m, d)], axis=0)
    cvec = jnp.zeros((SUBLANES, d), F32).at[:nb].set(c).at[nb].set(c_ctx)
    mods = _adaln(cvec, ada_w, ada_b).reshape(depth, SUBLANES, 6, d)

    tiles = np.arange(t_all // tm)
    lat_tile = tiles < t_lat // tm
    tile_mod = jnp.asarray(np.where(lat_tile, tiles // (n // tm), nb), jnp.int32)
    tile_pos = jnp.asarray(np.where(lat_tile, tiles % (n // tm), n // tm), jnp.int32)
    rtiles = np.arange(t_all // rt)
    route_mod = jnp.asarray(np.where(rtiles < t_lat // rt, rtiles // (n // rt), nb), jnp.int32)
    cos_t, sin_t = _rope_tables(n, grid_w, tm)
    gmat = jnp.asarray(np.kron(np.eye(ATT_Q_HEADS), np.ones((HEAD_DIM, HEAD_DIM))), BF16)
    dft_lat = _dft_constants(n)
    dft_ctx = _dft_constants(m)

    for l in range(depth):
        need_ctx = l < depth - 1
        n_tok = t_all if need_ctx else t_lat
        q, k, v, hyp, nq, nk, nv = _proj(
            xf, mods[l], w_in[l].astype(BF16), jnp.tile(q_gain[l], ATT_Q_HEADS)[None, :],
            jnp.tile(k_gain[l], ATT_KV_HEADS)[None, :], gmat, cos_t, sin_t, tile_mod, tile_pos)
        y_att = _gqa(q, k, v, nb, n, m, need_ctx)
        filt = (hy_w1[l], hy_b1[l], hy_freq[l], hy_w2[l], hy_b2[l], hy_w3[l], hy_b3[l])
        gre, gim = _hyena_filter_spectrum(n, dft_lat, *filt)
        cb = hy_conv_b[l][None, :]
        dsk = hy_d[l][None, :]
        y_hy = _hyena(hyp, hy_conv_w[l], cb, dsk, gre, gim, dft_lat, nb, n, 0)
        if need_ctx:
            gre_c, gim_c = _hyena_filter_spectrum(m, dft_ctx, *filt)
            y_hy_c = _hyena(hyp, hy_conv_w[l], cb, dsk, gre_c, gim_c, dft_ctx, nb, m, t_lat // m)
            y_hy = jnp.concatenate([y_hy, y_hy_c], axis=0)
        bias_tab = _na_bias_table(na_rpb[l], grid_w, n // grid_w)
        y_na = _na(nq, nk, nv, bias_tab, nb, n, m, grid_w, need_ctx)

        router_wp = jnp.zeros((d, LANES), F32).at[:, :n_experts].set(router_w[l])
        router_bp = jnp.zeros((1, LANES), F32).at[0, :n_experts].set(router_b[l])
        x1, h2p, idx, wts = _mix_out(xf, y_att, y_hy, y_na, mods[l], w_out[l].astype(BF16), ln1_g[l][None, :],
                                     ln1_b[l][None, :], router_wp, router_bp, tile_mod, n_tok, alpha, n_experts)
        slots, tile_expert, n_used, n_slots = _routing_tables(idx[:, :TOP_K], n_experts, MOE_TILE)
        xs = _dispatch(h2p, slots, n_slots)
        ys = _experts(xs, tile_expert, n_used, exp_w1.reshape((-1,) + exp_w1.shape[2:]), exp_b1[l],
                      exp_w2.reshape((-1,) + exp_w2.shape[2:]), exp_b2[l], l)
        xf = _combine(x1, wts, slots, ys, mods[l], ln2_g[l][None, :], ln2_b[l][None, :], route_mod, alpha)
    return xf[:t_lat].reshape(nb, n, d)


def kernel(x, c, ctx, c_ctx, ada_w, ada_b, w_in, w_out, q_gain, k_gain, hy_conv_w, hy_conv_b, hy_w1, hy_b1,
           hy_freq, hy_w2, hy_b2, hy_w3, hy_b3, hy_d, na_rpb, ln1_g, ln1_b, ln2_g, ln2_b, router_w, router_b,
           exp_w1, exp_b1, exp_w2, exp_b2):
    return _forward(x, c, ctx, c_ctx, ada_w, ada_b, w_in, w_out, q_gain, k_gain, hy_conv_w, hy_conv_b, hy_w1,
                    hy_b1, hy_freq, hy_w2, hy_b2, hy_w3, hy_b3, hy_d, na_rpb, ln1_g, ln1_b, ln2_g, ln2_b,
                    router_w, router_b, exp_w1, exp_b1, exp_w2, exp_b2, grid_w=64)
```
